```python
import math
import jax, jax.numpy as jnp
from jax import lax
import numpy as np

D_MODEL = 1024
BATCH = 2
SEQ = 16384
DEPTH = 4

N_MIXERS = 2
D_FF = 2816
HG_HEADS = 8
HG_KEY = D_MODEL // HG_HEADS
HG_VAL = D_MODEL // HG_HEADS
HG_CHUNK = 64
FOX_HEADS = 16
FOX_HEAD_DIM = D_MODEL // FOX_HEADS
Q_BLOCK = 128
N_HG_LAYERS = (DEPTH + 1) // 2
N_FOX_LAYERS = DEPTH // 2
EPS = 1e-6
MASK_VALUE = -1e30
MIN_GATE = 1e-30

kernel_name = "hgrn2_fox_macaron_interleaved"


def rmsnorm(x, g):
    x32 = x.astype(jnp.float32)
    y = x32 * lax.rsqrt(jnp.mean(x32 * x32, axis=-1, keepdims=True) + EPS)
    return (y * g.astype(jnp.float32)).astype(x.dtype)


def swiglu(h, w_in, w_out):
    gate, up = jnp.split(h @ w_in, 2, axis=-1)
    return (jax.nn.silu(gate) * up) @ w_out


def hgrn2_mixer(h, w_in, lb, out_norm_g, w_out):
    B, T, _ = h.shape
    C = HG_CHUNK
    n = T // C
    q, z_f, i, g = jnp.split(h @ w_in, 4, axis=-1)
    z32 = z_f.astype(jnp.float32)
    lb = lb.astype(jnp.float32)
    f = lb + (1.0 - lb) * jax.nn.sigmoid(z32)
    log_f = jnp.log(jnp.maximum(f, MIN_GATE))
    k = (1.0 - lb) * jax.nn.sigmoid(-z32)
    q = jax.nn.silu(q.astype(jnp.float32))
    v = i.astype(jnp.float32)

    def to_chunks(a, d):
        return a.reshape(B, n, C, HG_HEADS, d).transpose(1, 0, 3, 2, 4)

    qc, kc, lfc = to_chunks(q, HG_KEY), to_chunks(k, HG_KEY), to_chunks(log_f, HG_KEY)
    vc = to_chunks(v, HG_VAL)
    causal = jnp.tril(jnp.ones((C, C), dtype=bool))[:, :, None]

    def step(S, inp):
        qb, kb, vb, lb_c = inp
        b = jnp.cumsum(lb_c, axis=2)
        o_inter = jnp.einsum('bhtk,bhkv->bhtv', qb * jnp.exp(b), S)
        diff = b[:, :, :, None, :] - b[:, :, None, :, :]
        decay = jnp.where(causal, jnp.exp(jnp.where(causal, diff, 0.0)), 0.0)
        scores = jnp.einsum('bhtk,bhtsk,bhsk->bhts', qb, decay, kb)
        o_intra = jnp.einsum('bhts,bhsv->bhtv', scores, vb)
        b_last = b[:, :, -1:, :]
        k_dec = kb * jnp.exp(b_last - b)
        S_new = jnp.exp(b_last[:, :, 0, :])[..., None] * S + jnp.einsum('bhsk,bhsv->bhkv', k_dec, vb)
        return S_new, o_inter + o_intra

    S0 = jnp.zeros((B, HG_HEADS, HG_KEY, HG_VAL), jnp.float32)
    _, o = lax.scan(step, S0, (qc, kc, vc, lfc))
    o = o.transpose(1, 0, 3, 2, 4).reshape(B, T, HG_HEADS, HG_VAL)
    o = rmsnorm(o, out_norm_g).reshape(B, T, D_MODEL)
    o = o * jax.nn.silu(g.astype(jnp.float32))
    return o.astype(h.dtype) @ w_out


def fox_mixer(h, w_in, b_f, q_norm_g, k_norm_g, w_out):
    B, T, _ = h.shape
    H, Dh = FOX_HEADS, FOX_HEAD_DIM
    nb = T // Q_BLOCK
    D = D_MODEL
    proj = h @ w_in
    q, k, v, g, z_f = jnp.split(proj, [D, 2 * D, 3 * D, 4 * D], axis=-1)
    q = rmsnorm(q.reshape(B, T, H, Dh), q_norm_g).transpose(0, 2, 1, 3)
    k = rmsnorm(k.reshape(B, T, H, Dh), k_norm_g).transpose(0, 2, 1, 3)
    v = v.reshape(B, T, H, Dh).transpose(0, 2, 1, 3)
    log_f = jax.nn.log_sigmoid(z_f.astype(jnp.float32) + b_f.astype(jnp.float32))
    c = jnp.cumsum(log_f, axis=1).transpose(0, 2, 1)
    qb = q.reshape(B, H, nb, Q_BLOCK, Dh).transpose(2, 0, 1, 3, 4)
    cq = c.reshape(B, H, nb, Q_BLOCK).transpose(2, 0, 1, 3)
    scale = 1.0 / math.sqrt(Dh)
    key_pos = jnp.arange(T)

    def block(args):
        qi, cqi, start = args
        s = jnp.einsum('bhqd,bhkd->bhqk', qi, k).astype(jnp.float32) * scale
        s = s + cqi[..., None] - c[:, :, None, :]
        qpos = start + jnp.arange(Q_BLOCK)
        s = jnp.where(key_pos[None, :] <= qpos[:, None], s, MASK_VALUE)
        p = jax.nn.softmax(s, axis=-1)
        return jnp.einsum('bhqk,bhkd->bhqd', p.astype(v.dtype), v)

    o = lax.map(block, (qb, cq, jnp.arange(nb) * Q_BLOCK))
    o = o.transpose(1, 0, 3, 2, 4).reshape(B, T, D)
    o = o * jax.nn.sigmoid(g)
    return o @ w_out


def setup_inputs(seed: int = 0) -> dict:
    key = jax.random.key(seed)
    ks = jax.random.split(key, 16)
    f32 = jnp.float32
    D = D_MODEL

    def w(k, shape, fan_in):
        return jax.random.normal(k, shape, f32) * (fan_in ** -0.5)

    return {
        "x": jax.random.normal(ks[0], (BATCH, SEQ, D), f32),
        "norm_g": 1.0 + 0.05 * jax.random.normal(ks[1], (DEPTH, 3, D), f32),
        "ffn_w_in": w(ks[2], (DEPTH, 2, D, 2 * D_FF), D),
        "ffn_w_out": w(ks[3], (DEPTH, 2, D_FF, D), D_FF),
        "hg_w_in": w(ks[4], (N_HG_LAYERS, D, 4 * D), D),
        "hg_lb_logits": 0.5 * jax.random.normal(ks[5], (N_HG_LAYERS, HG_HEADS * HG_KEY), f32),
        "hg_out_norm_g": 1.0 + 0.05 * jax.random.normal(ks[6], (N_HG_LAYERS, HG_VAL), f32),
        "hg_w_out": w(ks[7], (N_HG_LAYERS, D, D), D),
        "fox_w_in": w(ks[8], (N_FOX_LAYERS, D, 4 * D + FOX_HEADS), D),
        "fox_b_f": 2.0 + 0.5 * jax.random.normal(ks[9], (N_FOX_LAYERS, FOX_HEADS), f32),
        "fox_q_norm_g": 1.0 + 0.05 * jax.random.normal(ks[10], (N_FOX_LAYERS, FOX_HEAD_DIM), f32),
        "fox_k_norm_g": 1.0 + 0.05 * jax.random.normal(ks[11], (N_FOX_LAYERS, FOX_HEAD_DIM), f32),
        "fox_w_out": w(ks[12], (N_FOX_LAYERS, D, D), D),
    }


def reference(x, norm_g, ffn_w_in, ffn_w_out, hg_w_in, hg_lb_logits, hg_out_norm_g, hg_w_out,
              fox_w_in, fox_b_f, fox_q_norm_g, fox_k_norm_g, fox_w_out):
    p = jax.nn.softmax(hg_lb_logits.astype(jnp.float32), axis=0)
    lbs = jnp.cumsum(p, axis=0) - p[0]
    for layer in range(DEPTH):
        j = layer // N_MIXERS
        x = x + 0.5 * swiglu(rmsnorm(x, norm_g[layer, 0]), ffn_w_in[layer, 0], ffn_w_out[layer, 0])
        hn = rmsnorm(x, norm_g[layer, 1])
        if layer % N_MIXERS == 0:
            x = x + hgrn2_mixer(hn, hg_w_in[j], lbs[j], hg_out_norm_g[j], hg_w_out[j])
        else:
            x = x + fox_mixer(hn, fox_w_in[j], fox_b_f[j], fox_q_norm_g[j], fox_k_norm_g[j], fox_w_out[j])
        x = x + 0.5 * swiglu(rmsnorm(x, norm_g[layer, 2]), ffn_w_in[layer, 1], ffn_w_out[layer, 1])
    return x
```

```python
import functools
import math

import jax
import jax.numpy as jnp
from jax import lax
from jax.experimental import pallas as pl
from jax.experimental.pallas import tpu as pltpu

F32 = jnp.float32
BF16 = jnp.bfloat16

EPS = 1e-6
MIN_GATE = 1e-30
MASK_VALUE = -1e30
HG_HEADS = 8
FOX_HEADS = 16
N_MIXERS = 2

LANES = 128
VMEM_LIMIT_BYTES = 56 * 1024 * 1024

_NT = (((1,), (1,)), ((), ()))
_TN = (((0,), (0,)), ((), ()))


def _params(*semantics):
    return pltpu.CompilerParams(dimension_semantics=semantics, vmem_limit_bytes=VMEM_LIMIT_BYTES)


def _rmsnorm(x, g):
    ms = jnp.mean(x * x, axis=-1, keepdims=True)
    return x * lax.rsqrt(ms + EPS) * g


def _silu(y):
    return y * jax.nn.sigmoid(y)


def _resident(shape):
    return pl.BlockSpec(shape, lambda *_: (0,) * len(shape), pipeline_mode=pl.Buffered(1))


def _ffn_body(x_ref, g_ref, win_ref, wout_ref, o_ref, hn_ref, a_ref, *, d_ff, tf):
    hn_ref[...] = _rmsnorm(x_ref[...], g_ref[...]).astype(BF16)
    for f in range(d_ff // tf):
        hn = hn_ref[...]
        gate = jnp.dot(hn, win_ref[:, f * tf:(f + 1) * tf], preferred_element_type=F32)
        up = jnp.dot(hn, win_ref[:, d_ff + f * tf:d_ff + (f + 1) * tf], preferred_element_type=F32)
        a_ref[:, f * tf:(f + 1) * tf] = (_silu(gate) * up).astype(BF16)
    y = jnp.dot(a_ref[...], wout_ref[...], preferred_element_type=F32)
    o_ref[...] = x_ref[...] + 0.5 * y


def _ffn(x, g, w_in, w_out):
    n, d = x.shape
    d_ff = w_out.shape[0]
    tm = min(512, n)
    tf = 256
    return pl.pallas_call(
        functools.partial(_ffn_body, d_ff=d_ff, tf=tf),
        grid=(n // tm,),
        in_specs=[
            pl.BlockSpec((tm, d), lambda i: (i, 0)),
            _resident((1, d)),
            _resident((d, 2 * d_ff)),
            _resident((d_ff, d)),
        ],
        out_specs=pl.BlockSpec((tm, d), lambda i: (i, 0)),
        out_shape=jax.ShapeDtypeStruct((n, d), F32),
        scratch_shapes=[pltpu.VMEM((tm, d), BF16), pltpu.VMEM((tm, d_ff), BF16)],
        compiler_params=_params("parallel"),
        name="ffn",
    )(x, g.reshape(1, d), w_in.astype(BF16), w_out.astype(BF16))


def _out_proj_body(x_ref, o_ref, w_ref, y_ref):
    y_ref[...] = x_ref[...] + jnp.dot(o_ref[...], w_ref[...], preferred_element_type=F32)


def _out_proj(x, o, w):
    n, d = x.shape
    tm = min(1024, n)
    return pl.pallas_call(
        _out_proj_body,
        grid=(n // tm,),
        in_specs=[
            pl.BlockSpec((tm, d), lambda i: (i, 0)),
            pl.BlockSpec((tm, d), lambda i: (i, 0)),
            _resident((d, d)),
        ],
        out_specs=pl.BlockSpec((tm, d), lambda i: (i, 0)),
        out_shape=jax.ShapeDtypeStruct((n, d), F32),
        compiler_params=_params("parallel"),
        name="out_proj",
    )(x, o, w.astype(BF16))


def _hg_proj_body(x_ref, g_ref, w_ref, lbl_ref, q_ref, k_ref, v_ref, lf_ref, gs_ref, hn_ref,
                  *, layer_j, tn):
    d = x_ref.shape[-1]
    hn_ref[...] = _rmsnorm(x_ref[...], g_ref[...]).astype(BF16)
    n_layers = lbl_ref.shape[0]
    rows = [lbl_ref[i:i + 1, :] for i in range(n_layers)]
    mx = functools.reduce(jnp.maximum, rows)
    es = [jnp.exp(r - mx) for r in rows]
    den = functools.reduce(jnp.add, es)
    ps = [e / den for e in es]
    lb = functools.reduce(jnp.add, ps[:layer_j + 1]) - ps[0]
    for part in range(4):
        for c in range(d // tn):
            y = jnp.dot(hn_ref[...], w_ref[:, part * d + c * tn:part * d + (c + 1) * tn],
                        preferred_element_type=F32)
            oc = slice(c * tn, (c + 1) * tn)
            if part == 0:
                q_ref[:, oc] = _silu(y)
            elif part == 1:
                lbc = lb[:, oc]
                f = lbc + (1.0 - lbc) * jax.nn.sigmoid(y)
                lf_ref[:, oc] = jnp.log(jnp.maximum(f, MIN_GATE))
                k_ref[:, oc] = (1.0 - lbc) * jax.nn.sigmoid(-y)
            elif part == 2:
                v_ref[:, oc] = y
            else:
                gs_ref[:, oc] = _silu(y)


def _hg_proj(x, g, w_in, lb_logits, layer_j):
    n, d = x.shape
    tm = min(512, n)
    row = pl.BlockSpec((tm, d), lambda i: (i, 0))
    out = jax.ShapeDtypeStruct((n, d), F32)
    return pl.pallas_call(
        functools.partial(_hg_proj_body, layer_j=layer_j, tn=512),
        grid=(n // tm,),
        in_specs=[row, _resident((1, d)), _resident((d, 4 * d)), _resident(lb_logits.shape)],
        out_specs=[row] * 5,
        out_shape=[out] * 5,
        scratch_shapes=[pltpu.VMEM((tm, d), BF16)],
        compiler_params=_params("parallel"),
        name="hg_proj",
    )(x, g.reshape(1, d), w_in.astype(BF16), lb_logits)


HG_CHUNK = 128
HG_BAND = 8


def _hg_rec_body(q_ref, k_ref, v_ref, lf_ref, gs_ref, gn_ref, o_ref, st_ref, *, tb):
    c_len = HG_CHUNK
    dk = q_ref.shape[-1]

    @pl.when(pl.program_id(2) == 0)
    def _():
        st_ref[...] = jnp.zeros_like(st_ref)

    row = lax.broadcasted_iota(jnp.int32, (c_len, 1), 0)
    rr = lax.broadcasted_iota(jnp.int32, (c_len, c_len), 0)
    cc = lax.broadcasted_iota(jnp.int32, (c_len, c_len), 1)
    tri = (cc <= rr).astype(BF16)
    levels = []
    m = c_len // 2
    while m >= HG_BAND:
        levels.append(m)
        m //= 2

    def chunk(c, carry):
        r0 = pl.multiple_of(c * c_len, c_len)
        q = q_ref[0, pl.ds(r0, c_len), :]
        k = k_ref[0, pl.ds(r0, c_len), :]
        v = v_ref[0, pl.ds(r0, c_len), :]
        lf = lf_ref[0, pl.ds(r0, c_len), :]
        gs = gs_ref[0, pl.ds(r0, c_len), :]
        lf_hi = lf.astype(BF16)
        lf_lo = (lf - lf_hi.astype(F32)).astype(BF16)
        b = (jnp.dot(tri, lf_hi, preferred_element_type=F32)
             + jnp.dot(tri, lf_lo, preferred_element_type=F32))
        b_last = b[c_len - 1:c_len, :]
        st = st_ref[...]
        o = lax.dot_general((q * jnp.exp(b)).astype(BF16), st.astype(BF16), _NT,
                            preferred_element_type=F32)
        sc = jnp.zeros((c_len, c_len), F32)
        for m in levels:
            b3 = b.reshape(c_len // (2 * m), 2 * m, dk)
            ref = jnp.broadcast_to(b3[:, m:m + 1, :], b3.shape).reshape(c_len, dk)
            is_q = (row % (2 * m)) >= m
            e = jnp.exp(jnp.where(is_q, b - ref, ref - b))
            s_m = lax.dot_general((q * e).astype(BF16), (k * e).astype(BF16), _NT,
                                  preferred_element_type=F32)
            valid = ((rr // (2 * m)) == (cc // (2 * m))) & ((rr % (2 * m)) >= m) & ((cc % (2 * m)) < m)
            sc = sc + jnp.where(valid, s_m, 0.0)
        o = o + jnp.dot(sc.astype(BF16), v.astype(BF16), preferred_element_type=F32)
        o = o + jnp.sum(q * k, axis=-1, keepdims=True) * v
        for dd in range(1, HG_BAND):
            kd = pltpu.roll(k, dd, 0)
            bd = pltpu.roll(b, dd, 0)
            vd = pltpu.roll(v, dd, 0)
            w = jnp.exp(jnp.minimum(b - bd, 0.0))
            s_d = jnp.sum(q * kd * w, axis=-1, keepdims=True)
            o = o + jnp.where((row % HG_BAND) >= dd, s_d, 0.0) * vd
        k_dec = (k * jnp.exp(b_last - b)).astype(BF16)
        st_ref[...] = st * jnp.exp(b_last) + lax.dot_general(v.astype(BF16), k_dec, _TN,
                                                             preferred_element_type=F32)
        ms = jnp.mean(o * o, axis=-1, keepdims=True)
        o_ref[0, pl.ds(r0, c_len), :] = (o * lax.rsqrt(ms + EPS) * gn_ref[...] * gs).astype(o_ref.dtype)
        return carry

    lax.fori_loop(0, tb // c_len, chunk, 0)


def _hg_rec(q, k, v, lf, gs, gn):
    bsz, t, d = q.shape
    dk = d // HG_HEADS
    tb = min(2048, t)
    blk = pl.BlockSpec((1, tb, dk), lambda b, h, i: (b, i, h))
    return pl.pallas_call(
        functools.partial(_hg_rec_body, tb=tb),
        grid=(bsz, HG_HEADS, t // tb),
        in_specs=[blk] * 5 + [pl.BlockSpec((1, dk), lambda b, h, i: (0, 0))],
        out_specs=blk,
        out_shape=jax.ShapeDtypeStruct((bsz, t, d), BF16),
        scratch_shapes=[pltpu.VMEM((dk, dk), F32)],
        compiler_params=_params("parallel", "parallel", "arbitrary"),
        name="hg_rec",
    )(q, k, v, lf, gs, gn.reshape(1, dk))


def _hgrn_layer(x, bsz, t, g, w_in, lb_logits, layer_j, out_norm_g, w_out):
    n, d = x.shape
    q, k, v, lf, gs = _hg_proj(x, g, w_in, lb_logits, layer_j)
    sh = (bsz, t, d)
    o = _hg_rec(q.reshape(sh), k.reshape(sh), v.reshape(sh), lf.reshape(sh), gs.reshape(sh), out_norm_g)
    return _out_proj(x, o.reshape(n, d), w_out)


def _head_rmsnorm(y, g2, ones2, hd):
    sq = y * y
    hi = sq.astype(BF16)
    lo = (sq - hi.astype(F32)).astype(BF16)
    ss = jnp.dot(hi, ones2, preferred_element_type=F32) + jnp.dot(lo, ones2, preferred_element_type=F32)
    return y * lax.rsqrt(ss * (1.0 / hd) + EPS) * g2


def _fox_proj_body(x_ref, g_ref, w_ref, wf_ref, bf_ref, gq_ref, gk_ref,
                   q_ref, k_ref, v_ref, sg_ref, lf_ref, hn_ref, *, hd):
    d = x_ref.shape[-1]
    hn_ref[...] = _rmsnorm(x_ref[...], g_ref[...]).astype(BF16)
    ri = lax.broadcasted_iota(jnp.int32, (LANES, LANES), 0)
    ci = lax.broadcasted_iota(jnp.int32, (LANES, LANES), 1)
    ones2 = ((ri // hd) == (ci // hd)).astype(BF16)
    for part in range(4):
        for c in range(d // LANES):
            y = jnp.dot(hn_ref[...], w_ref[:, part * d + c * LANES:part * d + (c + 1) * LANES],
                        preferred_element_type=F32)
            oc = slice(c * LANES, (c + 1) * LANES)
            if part == 0:
                q_ref[:, oc] = _head_rmsnorm(y, gq_ref[...], ones2, hd).astype(BF16)
            elif part == 1:
                k_ref[:, oc] = _head_rmsnorm(y, gk_ref[...], ones2, hd).astype(BF16)
            elif part == 2:
                v_ref[:, oc] = y.astype(BF16)
            else:
                sg_ref[:, oc] = jax.nn.sigmoid(y)
    z = jnp.dot(hn_ref[...], wf_ref[...], preferred_element_type=F32) + bf_ref[...]
    lf_ref[...] = jnp.minimum(z, 0.0) - jnp.log1p(jnp.exp(-jnp.abs(z)))


def _fox_proj(x, g, w_in, b_f, gq, gk):
    n, d = x.shape
    hd = d // FOX_HEADS
    tm = min(512, n)
    scale = 1.0 / math.sqrt(hd)
    rep = LANES // hd
    row = pl.BlockSpec((tm, d), lambda i: (i, 0))
    rowh = pl.BlockSpec((tm, FOX_HEADS), lambda i: (i, 0))
    return pl.pallas_call(
        functools.partial(_fox_proj_body, hd=hd),
        grid=(n // tm,),
        in_specs=[row, _resident((1, d)), _resident((d, 4 * d)), _resident((d, FOX_HEADS)),
                  _resident((1, FOX_HEADS)), _resident((1, LANES)), _resident((1, LANES))],
        out_specs=[row, row, row, row, rowh],
        out_shape=[jax.ShapeDtypeStruct((n, d), BF16)] * 3
        + [jax.ShapeDtypeStruct((n, d), F32), jax.ShapeDtypeStruct((n, FOX_HEADS), F32)],
        scratch_shapes=[pltpu.VMEM((tm, d), BF16)],
        compiler_params=_params("parallel"),
        name="fox_proj",
    )(x, g.reshape(1, d), w_in[:, :4 * d].astype(BF16), w_in[:, 4 * d:].astype(BF16),
      b_f.reshape(1, FOX_HEADS), jnp.tile(gq * scale, rep).reshape(1, LANES),
      jnp.tile(gk, rep).reshape(1, LANES))


def _cumsum_body(x_ref, o_ref):
    nb, r, _ = x_ref.shape
    ri = lax.broadcasted_iota(jnp.int32, (LANES, LANES), 0)
    ci = lax.broadcasted_iota(jnp.int32, (LANES, LANES), 1)
    upper = (ri <= ci).astype(BF16)

    def step(j, carry):
        x = x_ref[j]
        hi = x.astype(BF16)
        r1 = x - hi.astype(F32)
        mid = r1.astype(BF16)
        lo = (r1 - mid.astype(F32)).astype(BF16)
        loc = (jnp.dot(hi, upper, preferred_element_type=F32)
               + jnp.dot(mid, upper, preferred_element_type=F32)
               + jnp.dot(lo, upper, preferred_element_type=F32))
        o_ref[j] = loc + carry
        return carry + loc[:, LANES - 1:LANES]

    lax.fori_loop(0, nb, step, jnp.zeros((r, 1), F32))


def _cumsum_time(lf_rows):
    r, t = lf_rows.shape
    nb = t // LANES
    x = lf_rows.reshape(r, nb, LANES).transpose(1, 0, 2)
    c = pl.pallas_call(
        _cumsum_body,
        out_shape=jax.ShapeDtypeStruct((nb, r, LANES), F32),
        compiler_params=pltpu.CompilerParams(vmem_limit_bytes=VMEM_LIMIT_BYTES),
        name="fox_cumsum",
    )(x)
    return c.transpose(1, 0, 2).reshape(r, t)


def _fox_attn_body(q_ref, k_ref, v_ref, cq_ref, ck_ref, sg_ref, o_ref, *, tq, hd):
    i = pl.program_id(2)
    q = q_ref[0]
    lane = lax.broadcasted_iota(jnp.int32, (tq, LANES), 1)
    rr = lax.broadcasted_iota(jnp.int32, (tq, tq), 0)
    cc = lax.broadcasted_iota(jnp.int32, (tq, tq), 1)
    outs = []
    for hh in range(LANES // hd):
        in_head = (lane // hd) == hh
        qh = jnp.where(in_head, q, jnp.zeros_like(q))
        cq = cq_ref[0, 0][:, hh:hh + 1]

        def scores(j):
            r0 = pl.multiple_of(j * tq, tq)
            kb = k_ref[0, pl.ds(r0, tq), :]
            vb = v_ref[0, pl.ds(r0, tq), :]
            s = lax.dot_general(qh, kb, _NT, preferred_element_type=F32)
            s = s + cq - ck_ref[0, 0, hh:hh + 1, pl.ds(r0, tq)]
            return s, vb

        def update(carry, s, vb):
            m, l, acc = carry
            m_new = jnp.maximum(m, jnp.max(s, axis=-1, keepdims=True))
            alpha = jnp.exp(m - m_new)
            p = jnp.exp(s - m_new)
            l = alpha * l + jnp.sum(p, axis=-1, keepdims=True)
            acc = alpha * acc + jnp.dot(p.astype(BF16), vb, preferred_element_type=F32)
            return m_new, l, acc

        def body(j, carry):
            s, vb = scores(j)
            return update(carry, s, vb)

        init = (jnp.full((tq, 1), MASK_VALUE, F32), jnp.zeros((tq, 1), F32), jnp.zeros((tq, LANES), F32))
        carry = lax.fori_loop(0, i, body, init)
        s, vb = scores(i)
        s = jnp.where(cc <= rr, s, MASK_VALUE)
        _, l, acc = update(carry, s, vb)
        outs.append((acc / l, in_head))
    o = outs[0][0]
    for oh, in_head in outs[1:]:
        o = jnp.where(in_head, oh, o)
    o_ref[0] = (o * sg_ref[0]).astype(o_ref.dtype)


def _fox_attn(q, k, v, c, sg):
    bsz, t, d = q.shape
    hd = d // FOX_HEADS
    hpb = LANES // hd
    nblk = d // LANES
    tq = min(512, t)
    ck = c.reshape(bsz, nblk, hpb, t)
    cq = ck.transpose(0, 1, 3, 2)
    qblk = pl.BlockSpec((1, tq, LANES), lambda b, p, i: (b, i, p))
    full = pl.BlockSpec((1, t, LANES), lambda b, p, i: (b, 0, p))
    return pl.pallas_call(
        functools.partial(_fox_attn_body, tq=tq, hd=hd),
        grid=(bsz, nblk, t // tq),
        in_specs=[qblk, full, full,
                  pl.BlockSpec((1, 1, tq, hpb), lambda b, p, i: (b, p, i, 0)),
                  pl.BlockSpec((1, 1, hpb, t), lambda b, p, i: (b, p, 0, 0)),
                  qblk],
        out_specs=qblk,
        out_shape=jax.ShapeDtypeStruct((bsz, t, d), BF16),
        compiler_params=_params("parallel", "parallel", "arbitrary"),
        name="fox_attn",
    )(q, k, v, cq, ck, sg)


def _fox_layer(x, bsz, t, g, w_in, b_f, gq, gk, w_out):
    n, d = x.shape
    q, k, v, sg, lf = _fox_proj(x, g, w_in, b_f, gq, gk)
    lf_rows = lf.reshape(bsz, t, FOX_HEADS).transpose(0, 2, 1).reshape(bsz * FOX_HEADS, t)
    c = _cumsum_time(lf_rows).reshape(bsz, FOX_HEADS, t)
    sh = (bsz, t, d)
    o = _fox_attn(q.reshape(sh), k.reshape(sh), v.reshape(sh), c, sg.reshape(sh))
    return _out_proj(x, o.reshape(n, d), w_out)


def kernel(x, norm_g, ffn_w_in, ffn_w_out, hg_w_in, hg_lb_logits, hg_out_norm_g, hg_w_out,
           fox_w_in, fox_b_f, fox_q_norm_g, fox_k_norm_g, fox_w_out):
    bsz, t, d = x.shape
    depth = norm_g.shape[0]
    h = x.reshape(bsz * t, d)
    for layer in range(depth):
        j = layer // N_MIXERS
        h = _ffn(h, norm_g[layer, 0], ffn_w_in[layer, 0], ffn_w_out[layer, 0])
        if layer % N_MIXERS == 0:
            h = _hgrn_layer(h, bsz, t, norm_g[layer, 1], hg_w_in[j], hg_lb_logits, j,
                            hg_out_norm_g[j], hg_w_out[j])
        else:
            h = _fox_layer(h, bsz, t, norm_g[layer, 1], fox_w_in[j], fox_b_f[j],
                           fox_q_norm_g[j], fox_k_norm_g[j], fox_w_out[j])
        h = _ffn(h, norm_g[layer, 2], ffn_w_in[layer, 1], ffn_w_out[layer, 1])
    return h.reshape(bsz, t, d)
```

```python
import functools
import math

import jax
import jax.numpy as jnp
from jax import lax
from jax.experimental import pallas as pl
from jax.experimental.pallas import tpu as pltpu

F32 = jnp.float32
BF16 = jnp.bfloat16

EPS = 1e-6
MIN_GATE = 1e-30
MASK_VALUE = -1e30
HG_HEADS = 8
FOX_HEADS = 16
N_MIXERS = 2

LANES = 128
VMEM_LIMIT_BYTES = 56 * 1024 * 1024

_NT = (((1,), (1,)), ((), ()))
_TN = (((0,), (0,)), ((), ()))


def _params(*semantics):
    return pltpu.CompilerParams(dimension_semantics=semantics, vmem_limit_bytes=VMEM_LIMIT_BYTES)


def _rmsnorm(x, g):
    ms = jnp.mean(x * x, axis=-1, keepdims=True)
    return x * lax.rsqrt(ms + EPS) * g


def _silu(y):
    return y * jax.nn.sigmoid(y)


def _resident(shape):
    return pl.BlockSpec(shape, lambda *_: (0,) * len(shape), pipeline_mode=pl.Buffered(1))


def _ffn_body(x_ref, g_ref, win_ref, wout_ref, o_ref, hn_ref, a_ref, *, d_ff, tf):
    hn_ref[...] = _rmsnorm(x_ref[...], g_ref[...]).astype(BF16)
    for f in range(d_ff // tf):
        hn = hn_ref[...]
        gate = jnp.dot(hn, win_ref[:, f * tf:(f + 1) * tf], preferred_element_type=F32)
        up = jnp.dot(hn, win_ref[:, d_ff + f * tf:d_ff + (f + 1) * tf], preferred_element_type=F32)
        a_ref[:, f * tf:(f + 1) * tf] = (_silu(gate) * up).astype(BF16)
    y = jnp.dot(a_ref[...], wout_ref[...], preferred_element_type=F32)
    o_ref[...] = x_ref[...] + 0.5 * y


def _ffn(x, g, w_in, w_out):
    n, d = x.shape
    d_ff = w_out.shape[0]
    tm = min(512, n)
    tf = 256
    return pl.pallas_call(
        functools.partial(_ffn_body, d_ff=d_ff, tf=tf),
        grid=(n // tm,),
        in_specs=[
            pl.BlockSpec((tm, d), lambda i: (i, 0)),
            _resident((1, d)),
            _resident((d, 2 * d_ff)),
            _resident((d_ff, d)),
        ],
        out_specs=pl.BlockSpec((tm, d), lambda i: (i, 0)),
        out_shape=jax.ShapeDtypeStruct((n, d), F32),
        scratch_shapes=[pltpu.VMEM((tm, d), BF16), pltpu.VMEM((tm, d_ff), BF16)],
        compiler_params=_params("parallel"),
        name="ffn",
    )(x, g.reshape(1, d), w_in.astype(BF16), w_out.astype(BF16))


def _out_proj_body(x_ref, o_ref, w_ref, y_ref):
    y_ref[...] = x_ref[...] + jnp.dot(o_ref[...], w_ref[...], preferred_element_type=F32)


def _out_proj(x, o, w):
    n, d = x.shape
    tm = min(1024, n)
    return pl.pallas_call(
        _out_proj_body,
        grid=(n // tm,),
        in_specs=[
            pl.BlockSpec((tm, d), lambda i: (i, 0)),
            pl.BlockSpec((tm, d), lambda i: (i, 0)),
            _resident((d, d)),
        ],
        out_specs=pl.BlockSpec((tm, d), lambda i: (i, 0)),
        out_shape=jax.ShapeDtypeStruct((n, d), F32),
        compiler_params=_params("parallel"),
        name="out_proj",
    )(x, o, w.astype(BF16))


def _hg_proj_body(x_ref, g_ref, w_ref, lbl_ref, q_ref, k_ref, v_ref, lf_ref, gs_ref, hn_ref,
                  *, layer_j, tn):
    d = x_ref.shape[-1]
    hn_ref[...] = _rmsnorm(x_ref[...], g_ref[...]).astype(BF16)
    n_layers = lbl_ref.shape[0]
    rows = [lbl_ref[i:i + 1, :] for i in range(n_layers)]
    mx = functools.reduce(jnp.maximum, rows)
    es = [jnp.exp(r - mx) for r in rows]
    den = functools.reduce(jnp.add, es)
    ps = [e / den for e in es]
    lb = functools.reduce(jnp.add, ps[:layer_j + 1]) - ps[0]
    for part in range(4):
        for c in range(d // tn):
            y = jnp.dot(hn_ref[...], w_ref[:, part * d + c * tn:part * d + (c + 1) * tn],
                        preferred_element_type=F32)
            oc = slice(c * tn, (c + 1) * tn)
            if part == 0:
                q_ref[:, oc] = _silu(y)
            elif part == 1:
                lbc = lb[:, oc]
                f = lbc + (1.0 - lbc) * jax.nn.sigmoid(y)
                lf_ref[:, oc] = jnp.log(jnp.maximum(f, MIN_GATE))
                k_ref[:, oc] = (1.0 - lbc) * jax.nn.sigmoid(-y)
            elif part == 2:
                v_ref[:, oc] = y
            else:
                gs_ref[:, oc] = _silu(y)


def _hg_proj(x, g, w_in, lb_logits, layer_j):
    n, d = x.shape
    tm = min(512, n)
    row = pl.BlockSpec((tm, d), lambda i: (i, 0))
    out = jax.ShapeDtypeStruct((n, d), F32)
    return pl.pallas_call(
        functools.partial(_hg_proj_body, layer_j=layer_j, tn=512),
        grid=(n // tm,),
        in_specs=[row, _resident((1, d)), _resident((d, 4 * d)), _resident(lb_logits.shape)],
        out_specs=[row] * 5,
        out_shape=[out] * 5,
        scratch_shapes=[pltpu.VMEM((tm, d), BF16)],
        compiler_params=_params("parallel"),
        name="hg_proj",
    )(x, g.reshape(1, d), w_in.astype(BF16), lb_logits)


HG_CHUNK = 128
HG_BAND = 8


def _hg_rec_body(q_ref, k_ref, v_ref, lf_ref, gs_ref, gn_ref, o_ref, st_ref, *, tb):
    c_len = HG_CHUNK
    dk = q_ref.shape[-1]

    @pl.when(pl.program_id(2) == 0)
    def _():
        st_ref[...] = jnp.zeros_like(st_ref)

    row = lax.broadcasted_iota(jnp.int32, (c_len, 1), 0)
    rr = lax.broadcasted_iota(jnp.int32, (c_len, c_len), 0)
    cc = lax.broadcasted_iota(jnp.int32, (c_len, c_len), 1)
    tri = (cc <= rr).astype(BF16)
    levels = []
    m = c_len // 2
    while m >= HG_BAND:
        levels.append(m)
        m //= 2

    def chunk(c, carry):
        r0 = pl.multiple_of(c * c_len, c_len)
        q = q_ref[0, pl.ds(r0, c_len), :]
        k = k_ref[0, pl.ds(r0, c_len), :]
        v = v_ref[0, pl.ds(r0, c_len), :]
        lf = lf_ref[0, pl.ds(r0, c_len), :]
        gs = gs_ref[0, pl.ds(r0, c_len), :]
        lf_hi = lf.astype(BF16)
        lf_lo = (lf - lf_hi.astype(F32)).astype(BF16)
        b = (jnp.dot(tri, lf_hi, preferred_element_type=F32)
             + jnp.dot(tri, lf_lo, preferred_element_type=F32))
        b_last = b[c_len - 1:c_len, :]
        st = st_ref[...]
        o = lax.dot_general((q * jnp.exp(b)).astype(BF16), st.astype(BF16), _NT,
                            preferred_element_type=F32)
        sc = jnp.zeros((c_len, c_len), F32)
        for m in levels:
            b3 = b.reshape(c_len // (2 * m), 2 * m, dk)
            ref = jnp.broadcast_to(b3[:, m:m + 1, :], b3.shape).reshape(c_len, dk)
            is_q = (row % (2 * m)) >= m
            e = jnp.exp(jnp.where(is_q, b - ref, ref - b))
            s_m = lax.dot_general((q * e).astype(BF16), (k * e).astype(BF16), _NT,
                                  preferred_element_type=F32)
            valid = ((rr // (2 * m)) == (cc // (2 * m))) & ((rr % (2 * m)) >= m) & ((cc % (2 * m)) < m)
            sc = sc + jnp.where(valid, s_m, 0.0)
        o = o + jnp.dot(sc.astype(BF16), v.astype(BF16), preferred_element_type=F32)
        o = o + jnp.sum(q * k, axis=-1, keepdims=True) * v
        for dd in range(1, HG_BAND):
            kd = pltpu.roll(k, dd, 0)
            bd = pltpu.roll(b, dd, 0)
            vd = pltpu.roll(v, dd, 0)
            w = jnp.exp(jnp.minimum(b - bd, 0.0))
            s_d = jnp.sum(q * kd * w, axis=-1, keepdims=True)
            o = o + jnp.where((row % HG_BAND) >= dd, s_d, 0.0) * vd
        k_dec = (k * jnp.exp(b_last - b)).astype(BF16)
        st_ref[...] = st * jnp.exp(b_last) + lax.dot_general(v.astype(BF16), k_dec, _TN,
                                                             preferred_element_type=F32)
        ms = jnp.mean(o * o, axis=-1, keepdims=True)
        o_ref[0, pl.ds(r0, c_len), :] = (o * lax.rsqrt(ms + EPS) * gn_ref[...] * gs).astype(o_ref.dtype)
        return carry

    lax.fori_loop(0, tb // c_len, chunk, 0)


def _hg_rec(q, k, v, lf, gs, gn):
    bsz, t, d = q.shape
    dk = d // HG_HEADS
    tb = min(2048, t)
    blk = pl.BlockSpec((1, tb, dk), lambda b, h, i: (b, i, h))
    return pl.pallas_call(
        functools.partial(_hg_rec_body, tb=tb),
        grid=(bsz, HG_HEADS, t // tb),
        in_specs=[blk] * 5 + [pl.BlockSpec((1, dk), lambda b, h, i: (0, 0))],
        out_specs=blk,
        out_shape=jax.ShapeDtypeStruct((bsz, t, d), BF16),
        scratch_shapes=[pltpu.VMEM((dk, dk), F32)],
        compiler_params=_params("parallel", "parallel", "arbitrary"),
        name="hg_rec",
    )(q, k, v, lf, gs, gn.reshape(1, dk))


def _hgrn_layer(x, bsz, t, g, w_in, lb_logits, layer_j, out_norm_g, w_out):
    n, d = x.shape
    q, k, v, lf, gs = _hg_proj(x, g, w_in, lb_logits, layer_j)
    sh = (bsz, t, d)
    o = _hg_rec(q.reshape(sh), k.reshape(sh), v.reshape(sh), lf.reshape(sh), gs.reshape(sh), out_norm_g)
    return _out_proj(x, o.reshape(n, d), w_out)


def _head_rmsnorm(y, g2, ones2, hd):
    sq = y * y
    hi = sq.astype(BF16)
    lo = (sq - hi.astype(F32)).astype(BF16)
    ss = jnp.dot(hi, ones2, preferred_element_type=F32) + jnp.dot(lo, ones2, preferred_element_type=F32)
    return y * lax.rsqrt(ss * (1.0 / hd) + EPS) * g2


def _fox_proj_body(x_ref, g_ref, w_ref, wf_ref, bf_ref, gq_ref, gk_ref,
                   q_ref, k_ref, v_ref, sg_ref, lf_ref, hn_ref, *, hd):
    d = x_ref.shape[-1]
    hn_ref[...] = _rmsnorm(x_ref[...], g_ref[...]).astype(BF16)
    ri = lax.broadcasted_iota(jnp.int32, (LANES, LANES), 0)
    ci = lax.broadcasted_iota(jnp.int32, (LANES, LANES), 1)
    ones2 = ((ri // hd) == (ci // hd)).astype(BF16)
    for part in range(4):
        for c in range(d // LANES):
            y = jnp.dot(hn_ref[...], w_ref[:, part * d + c * LANES:part * d + (c + 1) * LANES],
                        preferred_element_type=F32)
            oc = slice(c * LANES, (c + 1) * LANES)
            if part == 0:
                q_ref[:, oc] = _head_rmsnorm(y, gq_ref[...], ones2, hd).astype(BF16)
            elif part == 1:
                k_ref[:, oc] = _head_rmsnorm(y, gk_ref[...], ones2, hd).astype(BF16)
            elif part == 2:
                v_ref[:, oc] = y.astype(BF16)
            else:
                sg_ref[:, oc] = jax.nn.sigmoid(y)
    z = jnp.dot(hn_ref[...], wf_ref[...], preferred_element_type=F32) + bf_ref[...]
    lf_ref[...] = jnp.minimum(z, 0.0) - jnp.log1p(jnp.exp(-jnp.abs(z)))


def _fox_proj(x, g, w_in, b_f, gq, gk):
    n, d = x.shape
    hd = d // FOX_HEADS
    tm = min(512, n)
    rep = LANES // hd
    row = pl.BlockSpec((tm, d), lambda i: (i, 0))
    rowh = pl.BlockSpec((tm, FOX_HEADS), lambda i: (i, 0))
    return pl.pallas_call(
        functools.partial(_fox_proj_body, hd=hd),
        grid=(n // tm,),
        in_specs=[row, _resident((1, d)), _resident((d, 4 * d)), _resident((d, FOX_HEADS)),
                  _resident((1, FOX_HEADS)), _resident((1, LANES)), _resident((1, LANES))],
        out_specs=[row, row, row, row, rowh],
        out_shape=[jax.ShapeDtypeStruct((n, d), BF16)] * 3
        + [jax.ShapeDtypeStruct((n, d), F32), jax.ShapeDtypeStruct((n, FOX_HEADS), F32)],
        scratch_shapes=[pltpu.VMEM((tm, d), BF16)],
        compiler_params=_params("parallel"),
        name="fox_proj",
    )(x, g.reshape(1, d), w_in[:, :4 * d].astype(BF16), w_in[:, 4 * d:].astype(BF16),
      b_f.reshape(1, FOX_HEADS), jnp.tile(gq, rep).reshape(1, LANES),
      jnp.tile(gk, rep).reshape(1, LANES))


def _cumsum_body(x_ref, o_ref):
    nb, r, _ = x_ref.shape
    ri = lax.broadcasted_iota(jnp.int32, (LANES, LANES), 0)
    ci = lax.broadcasted_iota(jnp.int32, (LANES, LANES), 1)
    upper = (ri <= ci).astype(BF16)

    def step(j, carry):
        x = x_ref[j]
        hi = x.astype(BF16)
        r1 = x - hi.astype(F32)
        mid = r1.astype(BF16)
        lo = (r1 - mid.astype(F32)).astype(BF16)
        loc = (jnp.dot(hi, upper, preferred_element_type=F32)
               + jnp.dot(mid, upper, preferred_element_type=F32)
               + jnp.dot(lo, upper, preferred_element_type=F32))
        o_ref[j] = loc + carry
        return carry + loc[:, LANES - 1:LANES]

    lax.fori_loop(0, nb, step, jnp.zeros((r, 1), F32))


def _cumsum_time(lf_rows):
    r, t = lf_rows.shape
    nb = t // LANES
    x = lf_rows.reshape(r, nb, LANES).transpose(1, 0, 2)
    c = pl.pallas_call(
        _cumsum_body,
        out_shape=jax.ShapeDtypeStruct((nb, r, LANES), F32),
        compiler_params=pltpu.CompilerParams(vmem_limit_bytes=VMEM_LIMIT_BYTES),
        name="fox_cumsum",
    )(x)
    return c.transpose(1, 0, 2).reshape(r, t)


LOG2E = 1.4426950408889634
FOX_TQ = 512
FOX_TK = 256
FOX_V_ROWS = 80
FOX_SKIP_LOG = 105.0


def _fox_attn_body(cs_ref, ce_ref, bound_ref, qa_ref, ka_ref, va_ref, sg_ref, o_ref, *, tq, tk, hd, n_heads):
    b, pr, i = pl.program_id(0), pl.program_id(1), pl.program_id(2)
    hpb = qa_ref.shape[1]
    nq = pl.num_programs(2)
    nk = nq * (tq // tk)
    spb = tq // tk
    rr = lax.broadcasted_iota(jnp.int32, (tk, tq), 0)
    cc = lax.broadcasted_iota(jnp.int32, (tk, tq), 1)
    two_b = 2.0 * bound_ref[0]
    outs = []
    for hh in range(hpb):
        head = pr * hpb + hh
        qa = qa_ref[0, hh]
        c_start = cs_ref[(b * n_heads + head) * nq + i]
        ke_base = (b * n_heads + head) * nk

        def keep_going(j):
            prev = jnp.maximum(j - 1, 0)
            return (j > 0) & (c_start - ce_ref[ke_base + prev] + two_b >= -FOX_SKIP_LOG)

        lo = lax.while_loop(keep_going, lambda j: j - 1, i * spb)

        def step(j, carry, masked):
            m, acc = carry
            r0 = pl.multiple_of(j * tk, tk)
            kb = ka_ref[0, hh, pl.ds(r0, tk), :]
            s = jnp.dot(kb, qa, preferred_element_type=F32)
            if masked:
                s = jnp.where(r0 + rr <= i * tq + cc, s, MASK_VALUE)
            m_new = jnp.maximum(m, jnp.max(s, axis=0, keepdims=True))
            alpha = jnp.exp2(m - m_new)
            p = jnp.exp2(s - m_new).astype(BF16)
            vb = va_ref[0, hh, :, pl.ds(r0, tk)]
            acc = alpha * acc + jnp.dot(vb, p, preferred_element_type=F32)
            return m_new, acc

        carry = (jnp.full((1, tq), MASK_VALUE, F32), jnp.zeros((FOX_V_ROWS, tq), F32))
        carry = lax.fori_loop(lo, i * spb, lambda j, c: step(j, c, False), carry)
        for sb in range(spb):
            carry = step(i * spb + sb, carry, True)
        _, acc = carry
        outs.append(acc[:hd, :] / acc[hd:hd + 1, :])
    o = jnp.concatenate(outs, axis=0).T
    o_ref[0] = (o * sg_ref[0]).astype(o_ref.dtype)


def _split3(x):
    hi = x.astype(BF16)
    r1 = x - hi.astype(F32)
    mid = r1.astype(BF16)
    lo = (r1 - mid.astype(F32)).astype(BF16)
    return hi, mid, lo


def _fox_attn(q, k, v, c, sg, bound):
    bsz, t, d = q.shape
    nh = FOX_HEADS
    hd = d // nh
    hpb = LANES // hd
    tq, tk = min(FOX_TQ, t), min(FOX_TK, t)
    hi, mid, lo = _split3(c * LOG2E)
    one = jnp.ones_like(hi)
    zq = jnp.zeros((bsz, nh, LANES - hd - 6, t), BF16)
    qa = jnp.concatenate([q.reshape(bsz, t, nh, hd).transpose(0, 2, 3, 1),
                          jnp.stack([hi, mid, lo, one, one, one], axis=2), zq], axis=2)
    ka = jnp.concatenate([k.reshape(bsz, t, nh, hd).transpose(0, 2, 1, 3),
                          jnp.stack([one, one, one, -hi, -mid, -lo], axis=3),
                          zq.transpose(0, 1, 3, 2)], axis=3)
    va = jnp.concatenate([v.reshape(bsz, t, nh, hd).transpose(0, 2, 3, 1),
                          jnp.ones((bsz, nh, FOX_V_ROWS - hd, t), BF16)], axis=2)
    c_start = c[:, :, 0::tq].reshape(-1)
    c_end = c[:, :, tk - 1::tk].reshape(-1)
    smem = pl.BlockSpec(memory_space=pltpu.SMEM)
    oblk = pl.BlockSpec((1, tq, LANES), lambda b, p, i: (b, i, p))
    return pl.pallas_call(
        functools.partial(_fox_attn_body, tq=tq, tk=tk, hd=hd, n_heads=nh),
        grid=(bsz, nh // hpb, t // tq),
        in_specs=[smem, smem, smem,
                  pl.BlockSpec((1, hpb, LANES, tq), lambda b, p, i: (b, p, 0, i)),
                  pl.BlockSpec((1, hpb, t, LANES), lambda b, p, i: (b, p, 0, 0)),
                  pl.BlockSpec((1, hpb, FOX_V_ROWS, t), lambda b, p, i: (b, p, 0, 0)),
                  oblk],
        out_specs=oblk,
        out_shape=jax.ShapeDtypeStruct((bsz, t, d), BF16),
        compiler_params=_params("parallel", "parallel", "arbitrary"),
        name="fox_attn",
    )(c_start, c_end, bound.reshape(1), qa, ka, va, sg)


def _fox_layer(x, bsz, t, g, w_in, b_f, gq, gk, w_out):
    n, d = x.shape
    hd = d // FOX_HEADS
    gq_s = gq * (LOG2E / math.sqrt(hd))
    q, k, v, sg, lf = _fox_proj(x, g, w_in, b_f, gq_s, gk)
    lf_rows = lf.reshape(bsz, t, FOX_HEADS).transpose(0, 2, 1).reshape(bsz * FOX_HEADS, t)
    c = _cumsum_time(lf_rows).reshape(bsz, FOX_HEADS, t)
    bound = 1.02 * hd * jnp.max(jnp.abs(gq)) / math.sqrt(hd) * jnp.max(jnp.abs(gk))
    sh = (bsz, t, d)
    o = _fox_attn(q.reshape(sh), k.reshape(sh), v.reshape(sh), c, sg.reshape(sh), bound)
    return _out_proj(x, o.reshape(n, d), w_out)


def kernel(x, norm_g, ffn_w_in, ffn_w_out, hg_w_in, hg_lb_logits, hg_out_norm_g, hg_w_out,
           fox_w_in, fox_b_f, fox_q_norm_g, fox_k_norm_g, fox_w_out):
    bsz, t, d = x.shape
    depth = norm_g.shape[0]
    h = x.reshape(bsz * t, d)
    for layer in range(depth):
        j = layer // N_MIXERS
        h = _ffn(h, norm_g[layer, 0], ffn_w_in[layer, 0], ffn_w_out[layer, 0])
        if layer % N_MIXERS == 0:
            h = _hgrn_layer(h, bsz, t, norm_g[layer, 1], hg_w_in[j], hg_lb_logits, j,
                            hg_out_norm_g[j], hg_w_out[j])
        else:
            h = _fox_layer(h, bsz, t, norm_g[layer, 1], fox_w_in[j], fox_b_f[j],
                           fox_q_norm_g[j], fox_k_norm_g[j], fox_w_out[j])
        h = _ffn(h, norm_g[layer, 2], ffn_w_in[layer, 1], ffn_w_out[layer, 1])
    return h.reshape(bsz, t, d)
```

```python
import functools
import math

import jax
import jax.numpy as jnp
from jax import lax
from jax.experimental import pallas as pl
from jax.experimental.pallas import tpu as pltpu

F32 = jnp.float32
BF16 = jnp.bfloat16

EPS = 1e-6
MIN_GATE = 1e-30
MASK_VALUE = -1e30
HG_HEADS = 8
FOX_HEADS = 16
N_MIXERS = 2

LANES = 128
VMEM_LIMIT_BYTES = 56 * 1024 * 1024

_NT = (((1,), (1,)), ((), ()))
_TN = (((0,), (0,)), ((), ()))


def _params(*semantics):
    return pltpu.CompilerParams(dimension_semantics=semantics, vmem_limit_bytes=VMEM_LIMIT_BYTES)


def _rmsnorm(x, g):
    ms = jnp.mean(x * x, axis=-1, keepdims=True)
    return x * lax.rsqrt(ms + EPS) * g


def _silu(y):
    return y * jax.nn.sigmoid(y)


def _resident(shape):
    return pl.BlockSpec(shape, lambda *_: (0,) * len(shape), pipeline_mode=pl.Buffered(1))


def _ffn_body(*refs, d_ff, tf, has_mixer):
    if has_mixer:
        x_ref, mo_ref, mw_ref, g_ref, win_ref, wout_ref, o_ref, hn_ref, a_ref, xr_ref = refs
        xr_ref[...] = x_ref[...] + jnp.dot(mo_ref[...], mw_ref[...], preferred_element_type=F32)
        x_ref = xr_ref
    else:
        x_ref, g_ref, win_ref, wout_ref, o_ref, hn_ref, a_ref = refs
    hn_ref[...] = _rmsnorm(x_ref[...], g_ref[...]).astype(BF16)
    for f in range(d_ff // tf):
        hn = hn_ref[...]
        gate = jnp.dot(hn, win_ref[:, f * tf:(f + 1) * tf], preferred_element_type=F32)
        up = jnp.dot(hn, win_ref[:, d_ff + f * tf:d_ff + (f + 1) * tf], preferred_element_type=F32)
        a_ref[:, f * tf:(f + 1) * tf] = (_silu(gate) * up).astype(BF16)
    y = jnp.dot(a_ref[...], wout_ref[...], preferred_element_type=F32)
    o_ref[...] = x_ref[...] + 0.5 * y


def _ffn(x, g, w_in, w_out, mixer=None):
    n, d = x.shape
    d_ff = w_out.shape[0]
    tm = min(512, n)
    tf = 256
    row = pl.BlockSpec((tm, d), lambda i: (i, 0))
    args, specs, scratch = [x], [row], [pltpu.VMEM((tm, d), BF16), pltpu.VMEM((tm, d_ff), BF16)]
    if mixer is not None:
        args += [mixer[0], mixer[1].astype(BF16)]
        specs += [row, _resident((d, d))]
        scratch.append(pltpu.VMEM((tm, d), F32))
    return pl.pallas_call(
        functools.partial(_ffn_body, d_ff=d_ff, tf=tf, has_mixer=mixer is not None),
        grid=(n // tm,),
        in_specs=specs + [_resident((1, d)), _resident((d, 2 * d_ff)), _resident((d_ff, d))],
        out_specs=row,
        out_shape=jax.ShapeDtypeStruct((n, d), F32),
        scratch_shapes=scratch,
        compiler_params=_params("parallel"),
        name="ffn",
    )(*args, g.reshape(1, d), w_in.astype(BF16), w_out.astype(BF16))


def _hg_proj_body(x_ref, g_ref, w_ref, lbl_ref, q_ref, k_ref, v_ref, lf_ref, gs_ref, hn_ref,
                  *, layer_j, tn):
    d = x_ref.shape[-1]
    hn_ref[...] = _rmsnorm(x_ref[...], g_ref[...]).astype(BF16)
    n_layers = lbl_ref.shape[0]
    rows = [lbl_ref[i:i + 1, :] for i in range(n_layers)]
    mx = functools.reduce(jnp.maximum, rows)
    es = [jnp.exp(r - mx) for r in rows]
    den = functools.reduce(jnp.add, es)
    ps = [e / den for e in es]
    lb = functools.reduce(jnp.add, ps[:layer_j + 1]) - ps[0]
    for part in range(4):
        for c in range(d // tn):
            y = jnp.dot(hn_ref[...], w_ref[:, part * d + c * tn:part * d + (c + 1) * tn],
                        preferred_element_type=F32)
            oc = slice(c * tn, (c + 1) * tn)
            if part == 0:
                q_ref[:, oc] = _silu(y)
            elif part == 1:
                lbc = lb[:, oc]
                f = lbc + (1.0 - lbc) * jax.nn.sigmoid(y)
                lf_ref[:, oc] = jnp.log(jnp.maximum(f, MIN_GATE))
                k_ref[:, oc] = (1.0 - lbc) * jax.nn.sigmoid(-y)
            elif part == 2:
                v_ref[:, oc] = y
            else:
                gs_ref[:, oc] = _silu(y)


def _hg_proj(x, g, w_in, lb_logits, layer_j):
    n, d = x.shape
    tm = min(512, n)
    row = pl.BlockSpec((tm, d), lambda i: (i, 0))
    out = jax.ShapeDtypeStruct((n, d), F32)
    return pl.pallas_call(
        functools.partial(_hg_proj_body, layer_j=layer_j, tn=512),
        grid=(n // tm,),
        in_specs=[row, _resident((1, d)), _resident((d, 4 * d)), _resident(lb_logits.shape)],
        out_specs=[row] * 5,
        out_shape=[out] * 5,
        scratch_shapes=[pltpu.VMEM((tm, d), BF16)],
        compiler_params=_params("parallel"),
        name="hg_proj",
    )(x, g.reshape(1, d), w_in.astype(BF16), lb_logits)


HG_CHUNK = 128
HG_BAND = 4
SUBLANES = 8


def _hg_rec_body(q_ref, k_ref, v_ref, lf_ref, gs_ref, gn_ref, o_ref, st_ref, *, tb):
    c_len = HG_CHUNK
    dk = q_ref.shape[-1]

    @pl.when(pl.program_id(2) == 0)
    def _():
        st_ref[...] = jnp.zeros_like(st_ref)

    row = lax.broadcasted_iota(jnp.int32, (c_len, 1), 0)
    rr = lax.broadcasted_iota(jnp.int32, (c_len, c_len), 0)
    cc = lax.broadcasted_iota(jnp.int32, (c_len, c_len), 1)
    tri = (cc <= rr).astype(BF16)
    levels = []
    m = c_len // 2
    while m >= HG_BAND:
        levels.append(m)
        m //= 2
    diff = jnp.where(cc < rr, rr ^ cc, 0)
    pair_level = functools.reduce(jnp.add, [(diff >= m).astype(jnp.int32) for m in levels])
    band_ok = [(row % HG_BAND) >= dd for dd in range(HG_BAND)]

    def roll_in_tile(a, dd):
        a3 = a.reshape(c_len // SUBLANES, SUBLANES, dk)
        return pltpu.roll(a3, dd, 1).reshape(c_len, dk)

    def chunk(c, carry):
        r0 = pl.multiple_of(c * c_len, c_len)
        q = q_ref[0, pl.ds(r0, c_len), :]
        k = k_ref[0, pl.ds(r0, c_len), :]
        v = v_ref[0, pl.ds(r0, c_len), :]
        lf = lf_ref[0, pl.ds(r0, c_len), :]
        gs = gs_ref[0, pl.ds(r0, c_len), :]
        lf_hi = lf.astype(BF16)
        lf_lo = (lf - lf_hi.astype(F32)).astype(BF16)
        b = (jnp.dot(tri, lf_hi, preferred_element_type=F32)
             + jnp.dot(tri, lf_lo, preferred_element_type=F32))
        b_last = b[c_len - 1:c_len, :]
        st = st_ref[...]
        o = lax.dot_general((q * jnp.exp(b)).astype(BF16), st.astype(BF16), _NT,
                            preferred_element_type=F32)
        sc = jnp.zeros((c_len, c_len), F32)
        for idx, m in enumerate(levels):
            b3 = b.reshape(c_len // (2 * m), 2 * m, dk)
            ref = jnp.broadcast_to(b3[:, m:m + 1, :], b3.shape).reshape(c_len, dk)
            e = jnp.exp(-jnp.abs(b - ref))
            s_m = lax.dot_general((q * e).astype(BF16), (k * e).astype(BF16), _NT,
                                  preferred_element_type=F32)
            sc = jnp.where(pair_level == len(levels) - idx, s_m, sc)
        o = o + jnp.dot(sc.astype(BF16), v.astype(BF16), preferred_element_type=F32)
        o = o + jnp.sum(q * k, axis=-1, keepdims=True) * v
        for dd in range(1, HG_BAND):
            kd = roll_in_tile(k, dd)
            bd = roll_in_tile(b, dd)
            vd = roll_in_tile(v, dd)
            w = jnp.exp(jnp.minimum(b - bd, 0.0))
            s_d = jnp.sum(q * kd * w, axis=-1, keepdims=True)
            o = o + jnp.where(band_ok[dd], s_d, 0.0) * vd
        k_dec = (k * jnp.exp(b_last - b)).astype(BF16)
        st_ref[...] = st * jnp.exp(b_last) + lax.dot_general(v.astype(BF16), k_dec, _TN,
                                                             preferred_element_type=F32)
        ms = jnp.mean(o * o, axis=-1, keepdims=True)
        o_ref[0, pl.ds(r0, c_len), :] = (o * lax.rsqrt(ms + EPS) * gn_ref[...] * gs).astype(o_ref.dtype)
        return carry

    lax.fori_loop(0, tb // c_len, chunk, 0, unroll=8)


def _hg_rec(q, k, v, lf, gs, gn):
    bsz, t, d = q.shape
    dk = d // HG_HEADS
    tb = min(2048, t)
    blk = pl.BlockSpec((1, tb, dk), lambda b, h, i: (b, i, h))
    return pl.pallas_call(
        functools.partial(_hg_rec_body, tb=tb),
        grid=(bsz, HG_HEADS, t // tb),
        in_specs=[blk] * 5 + [pl.BlockSpec((1, dk), lambda b, h, i: (0, 0))],
        out_specs=blk,
        out_shape=jax.ShapeDtypeStruct((bsz, t, d), BF16),
        scratch_shapes=[pltpu.VMEM((dk, dk), F32)],
        compiler_params=_params("parallel", "parallel", "arbitrary"),
        name="hg_rec",
    )(q, k, v, lf, gs, gn.reshape(1, dk))


def _hgrn_mixer(x, bsz, t, g, w_in, lb_logits, layer_j, out_norm_g):
    n, d = x.shape
    q, k, v, lf, gs = _hg_proj(x, g, w_in, lb_logits, layer_j)
    sh = (bsz, t, d)
    o = _hg_rec(q.reshape(sh), k.reshape(sh), v.reshape(sh), lf.reshape(sh), gs.reshape(sh), out_norm_g)
    return o.reshape(n, d)


def _head_rmsnorm(y, g2, ones2, hd):
    sq = y * y
    hi = sq.astype(BF16)
    lo = (sq - hi.astype(F32)).astype(BF16)
    ss = jnp.dot(hi, ones2, preferred_element_type=F32) + jnp.dot(lo, ones2, preferred_element_type=F32)
    return y * lax.rsqrt(ss * (1.0 / hd) + EPS) * g2


def _split3(x):
    hi = x.astype(BF16).astype(F32)
    r1 = x - hi
    mid = r1.astype(BF16).astype(F32)
    lo = (r1 - mid).astype(BF16).astype(F32)
    return hi, mid, lo


FOX_AUG_C = 0
FOX_AUG_ONE = 3
FOX_AUG_REF = 16
FOX_PIECE_STRIDE = 16
FOX_ONE_LANE = 48
FOX_V_ROWS = 80


def _fox_placement(nh, hd):
    import numpy as np
    hpb = LANES // hd
    assert hpb == 2 and 3 * FOX_PIECE_STRIDE <= FOX_ONE_LANE and nh <= FOX_PIECE_STRIDE
    pk = np.zeros((nh // hpb, LANES, hpb * LANES), np.float32)
    pq = np.zeros((nh // hpb, hpb * LANES, LANES), np.float32)
    for p in range(nh // hpb):
        for e in range(hpb):
            h = p * hpb + e
            base = e * LANES + (hd if e == 0 else 0)
            for i in range(3):
                pk[p, FOX_PIECE_STRIDE * i + h, base + FOX_AUG_ONE + i] = -1.0
                pk[p, FOX_ONE_LANE, base + FOX_AUG_C + i] = 1.0
                pk[p, FOX_ONE_LANE, base + FOX_AUG_REF + i] = 1.0
                pq[p, base + FOX_AUG_C + i, FOX_PIECE_STRIDE * i + h] = 1.0
                pq[p, base + FOX_AUG_ONE + i, FOX_ONE_LANE] = 1.0
    return jnp.asarray(pk, BF16), jnp.asarray(pq, BF16)


def _fox_proj_body(x_ref, g_ref, w_ref, wf_ref, bf_ref, gq_ref, gk_ref, pk_ref, pq_ref,
                   qa_ref, ka_ref, va_ref, sg_ref, c_ref, hn_ref, carry_ref, *, hd, nh):
    tm, d = x_ref.shape[1], x_ref.shape[2]
    hn_ref[...] = _rmsnorm(x_ref[0], g_ref[...]).astype(BF16)

    @pl.when(pl.program_id(1) == 0)
    def _():
        carry_ref[...] = jnp.zeros_like(carry_ref)

    z = jnp.dot(hn_ref[...], wf_ref[...], preferred_element_type=F32) + bf_ref[...]
    lf = jnp.minimum(z, 0.0) - jnp.log1p(jnp.exp(-jnp.abs(z)))
    tri = (lax.broadcasted_iota(jnp.int32, (tm, tm), 1)
           <= lax.broadcasted_iota(jnp.int32, (tm, tm), 0)).astype(BF16)
    c = carry_ref[...]
    for piece in _split3(lf):
        c = c + jnp.dot(tri, piece.astype(BF16), preferred_element_type=F32)
    carry_ref[...] = c[tm - 1:tm, :]
    c_ref[0] = c[:, :nh]
    hi, mid, lo = _split3(c * LOG2E)
    lane = lax.broadcasted_iota(jnp.int32, (tm, LANES), 1)
    pieces = jnp.where(lane < FOX_PIECE_STRIDE, hi,
                       jnp.where(lane < 2 * FOX_PIECE_STRIDE, mid,
                                 jnp.where(lane < 3 * FOX_PIECE_STRIDE, lo,
                                           jnp.where(lane == FOX_ONE_LANE, 1.0, 0.0))))
    pieces_t = pieces.T.astype(BF16)
    pieces = pieces.astype(BF16)

    ri = lax.broadcasted_iota(jnp.int32, (LANES, LANES), 0)
    ci = lax.broadcasted_iota(jnp.int32, (LANES, LANES), 1)
    ones2 = ((ri // hd) == (ci // hd)).astype(BF16)
    ones_rows = jnp.ones((FOX_V_ROWS - hd, tm), F32)
    hpb = LANES // hd
    wide = 2 * LANES
    for part in range(4):
        for cw in range(d // wide):
            yw = jnp.dot(hn_ref[...], w_ref[:, part * d + cw * wide:part * d + (cw + 1) * wide],
                         preferred_element_type=F32)
            if part == 3:
                sg_ref[0, :, cw * wide:(cw + 1) * wide] = jax.nn.sigmoid(yw)
                continue
            for half in range(wide // LANES):
                p = cw * (wide // LANES) + half
                y = yw[:, half * LANES:(half + 1) * LANES]
                if part == 1:
                    kn = _head_rmsnorm(y, gk_ref[...], ones2, hd)
                    aug = jnp.dot(pieces, pk_ref[p], preferred_element_type=F32)
                    for e in range(hpb):
                        ka_ref[0, p * hpb + e] = jnp.where((lane // hd) == e, kn,
                                                           aug[:, e * LANES:(e + 1) * LANES]).astype(BF16)
                    continue
                y_t = y.T
                if part == 2:
                    for e in range(hpb):
                        va_ref[0, p * hpb + e] = jnp.concatenate(
                            [y_t[e * hd:(e + 1) * hd], ones_rows], axis=0).astype(BF16)
                    continue
                aug_t = jnp.dot(pq_ref[p], pieces_t, preferred_element_type=F32)
                for e in range(hpb):
                    rows = y_t[e * hd:(e + 1) * hd]
                    ms = jnp.mean(rows * rows, axis=0, keepdims=True)
                    qn = rows * lax.rsqrt(ms + EPS) * gq_ref[...]
                    base = e * LANES + (hd if e == 0 else 0)
                    bias = aug_t[base:base + LANES - hd]
                    qa_ref[0, p * hpb + e] = jnp.concatenate([qn, bias] if e == 0 else [bias, qn],
                                                             axis=0).astype(BF16)


def _fox_proj(x, g, w_in, b_f, gq, gk):
    bsz, t, d = x.shape
    nh = FOX_HEADS
    hd = d // nh
    tm = min(512, t)
    pk, pq = _fox_placement(nh, hd)
    reps = FOX_ONE_LANE // FOX_PIECE_STRIDE
    w_f = jnp.pad(jnp.tile(jnp.pad(w_in[:, 4 * d:], ((0, 0), (0, FOX_PIECE_STRIDE - nh))), (1, reps)),
                  ((0, 0), (0, LANES - FOX_ONE_LANE)))
    b_pad = jnp.pad(jnp.tile(jnp.pad(b_f, (0, FOX_PIECE_STRIDE - nh)), reps), (0, LANES - FOX_ONE_LANE))
    row = pl.BlockSpec((1, tm, d), lambda b, i: (b, i, 0))
    return pl.pallas_call(
        functools.partial(_fox_proj_body, hd=hd, nh=nh),
        grid=(bsz, t // tm),
        in_specs=[row, _resident((1, d)), _resident((d, 4 * d)), _resident((d, LANES)),
                  _resident((1, LANES)), _resident((hd, 1)), _resident((1, LANES)),
                  _resident(pk.shape), _resident(pq.shape)],
        out_specs=[pl.BlockSpec((1, nh, LANES, tm), lambda b, i: (b, 0, 0, i)),
                   pl.BlockSpec((1, nh, tm, LANES), lambda b, i: (b, 0, i, 0)),
                   pl.BlockSpec((1, nh, FOX_V_ROWS, tm), lambda b, i: (b, 0, 0, i)),
                   row,
                   pl.BlockSpec((1, tm, nh), lambda b, i: (b, i, 0))],
        out_shape=[jax.ShapeDtypeStruct((bsz, nh, LANES, t), BF16),
                   jax.ShapeDtypeStruct((bsz, nh, t, LANES), BF16),
                   jax.ShapeDtypeStruct((bsz, nh, FOX_V_ROWS, t), BF16),
                   jax.ShapeDtypeStruct((bsz, t, d), F32),
                   jax.ShapeDtypeStruct((bsz, t, nh), F32)],
        scratch_shapes=[pltpu.VMEM((tm, d), BF16), pltpu.VMEM((1, LANES), F32)],
        compiler_params=_params("parallel", "arbitrary"),
        name="fox_proj",
    )(x, g.reshape(1, d), w_in[:, :4 * d].astype(BF16), w_f.astype(BF16), b_pad.reshape(1, LANES),
      gq.reshape(hd, 1), jnp.tile(gk, LANES // hd).reshape(1, LANES), pk, pq)


LOG2E = 1.4426950408889634
FOX_TQ = 512
FOX_TK = 256
FOX_SKIP_LOG = 105.0
FOX_FIXED_REF_MAX = 60.0


def _fox_blocks(cs_ref, ce_ref, bound_ref, qa_ref, ka_ref, e, *, tq, tk, hd, nh, ref_rows=None):
    b, pr, i = pl.program_id(0), pl.program_id(1), pl.program_id(2)
    spb = tq // tk
    nq = pl.num_programs(2)
    head = pr * (LANES // hd) + e
    qa = qa_ref[0, e]
    if ref_rows is not None:
        r0 = (hd if e == 0 else 0) + FOX_AUG_REF
        qa = jnp.concatenate([qa[:r0], ref_rows, qa[r0 + ref_rows.shape[0]:]], axis=0)
    c_start = cs_ref[(b * nq + i) * nh + head]
    two_b = 2.0 * bound_ref[0]

    def keep_going(j):
        prev = jnp.maximum(j - 1, 0)
        return (j > 0) & (c_start - ce_ref[(b * nq * spb + prev) * nh + head] + two_b >= -FOX_SKIP_LOG)

    lo = lax.while_loop(keep_going, lambda j: j - 1, i * spb)
    rr = lax.broadcasted_iota(jnp.int32, (tk, tq), 0)
    cc = lax.broadcasted_iota(jnp.int32, (tk, tq), 1)

    def scores(j, masked):
        r0 = pl.multiple_of(j * tk, tk)
        s = jnp.dot(ka_ref[0, e, pl.ds(r0, tk), :], qa, preferred_element_type=F32)
        if masked:
            s = jnp.where(r0 + rr <= i * tq + cc, s, MASK_VALUE)
        return s

    return lo, i * spb, spb, scores


def _fox_attn_body(cs_ref, ce_ref, bound_ref, qa_ref, mr_ref, ka_ref, va_ref, sg_ref, o_ref, *, tq, tk, hd, nh):
    outs = []
    for e in range(LANES // hd):
        lo, diag, spb, scores = _fox_blocks(cs_ref, ce_ref, bound_ref, qa_ref, ka_ref, e,
                                            tq=tq, tk=tk, hd=hd, nh=nh, ref_rows=mr_ref[0, e])

        def weighted(j, masked):
            p = jnp.exp2(scores(j, masked)).astype(BF16)
            vb = va_ref[0, e, :, pl.ds(pl.multiple_of(j * tk, tk), tk)]
            return jnp.dot(vb, p, preferred_element_type=F32)

        n_pairs = (diag - lo) // 2
        acc = lax.fori_loop(
            0, n_pairs,
            lambda u, a: a + (weighted(lo + 2 * u, False) + weighted(lo + 2 * u + 1, False)),
            jnp.zeros((FOX_V_ROWS, tq), F32))
        acc = lax.fori_loop(lo + 2 * n_pairs, diag, lambda j, a: a + weighted(j, False), acc)
        for sb in range(spb):
            acc = acc + weighted(diag + sb, True)
        outs.append(acc[:hd, :] / acc[hd:hd + 1, :])
    o = jnp.concatenate(outs, axis=0).T
    o_ref[0] = (o * sg_ref[0]).astype(o_ref.dtype)


def _fox_rowmax_body(cs_ref, ce_ref, bound_ref, qa_ref, ka_ref, m_ref, *, tq, tk, hd, nh):
    for e in range(LANES // hd):
        lo, diag, spb, scores = _fox_blocks(cs_ref, ce_ref, bound_ref, qa_ref, ka_ref, e,
                                            tq=tq, tk=tk, hd=hd, nh=nh)
        m = lax.fori_loop(lo, diag,
                          lambda j, m: jnp.maximum(m, jnp.max(scores(j, False), axis=0, keepdims=True)),
                          jnp.full((1, tq), MASK_VALUE, F32))
        for sb in range(spb):
            m = jnp.maximum(m, jnp.max(scores(diag + sb, True), axis=0, keepdims=True))
        m_ref[0, e] = jnp.broadcast_to(m, m_ref.shape[2:])


def _fox_attn(qa, ka, va, sg, c, bound):
    bsz, nh, _, t = qa.shape
    d = sg.shape[-1]
    hd = d // nh
    hpb = LANES // hd
    tq, tk = min(FOX_TQ, t), min(FOX_TK, t)
    c_start = c[:, 0::tq, :].reshape(-1)
    c_end = c[:, tk - 1::tk, :].reshape(-1)
    smem = pl.BlockSpec(memory_space=pltpu.SMEM)
    grid = (bsz, nh // hpb, t // tq)
    qblk = pl.BlockSpec((1, hpb, LANES, tq), lambda b, p, i: (b, p, 0, i))
    kblk = pl.BlockSpec((1, hpb, t, LANES), lambda b, p, i: (b, p, 0, 0))
    static = dict(tq=tq, tk=tk, hd=hd, nh=nh)
    scalars = (c_start, c_end, bound.reshape(1))
    bound2 = bound * LOG2E

    def true_rowmax():
        m = pl.pallas_call(
            functools.partial(_fox_rowmax_body, **static),
            grid=grid,
            in_specs=[smem, smem, smem, qblk, kblk],
            out_specs=pl.BlockSpec((1, hpb, 8, tq), lambda b, p, i: (b, p, 0, i)),
            out_shape=jax.ShapeDtypeStruct((bsz, nh, 8, t), F32),
            compiler_params=_params("parallel", "parallel", "arbitrary"),
            name="fox_rowmax",
        )(*scalars, qa, ka)
        return m[:, :, 0, :]

    ref = lax.cond(2.0 * bound2 <= FOX_FIXED_REF_MAX,
                   lambda: jnp.full((bsz, nh, t), bound2, F32), true_rowmax)
    pieces = jnp.stack([-p for p in _split3(ref)], axis=2)
    mr = jnp.pad(pieces, ((0, 0), (0, 0), (0, 16 - 3), (0, 0))).astype(BF16)
    oblk = pl.BlockSpec((1, tq, LANES), lambda b, p, i: (b, i, p))
    return pl.pallas_call(
        functools.partial(_fox_attn_body, **static),
        grid=grid,
        in_specs=[smem, smem, smem, qblk,
                  pl.BlockSpec((1, hpb, 16, tq), lambda b, p, i: (b, p, 0, i)),
                  kblk,
                  pl.BlockSpec((1, hpb, FOX_V_ROWS, t), lambda b, p, i: (b, p, 0, 0)),
                  oblk],
        out_specs=oblk,
        out_shape=jax.ShapeDtypeStruct((bsz, t, d), BF16),
        compiler_params=_params("parallel", "parallel", "arbitrary"),
        name="fox_attn",
    )(*scalars, qa, mr, ka, va, sg)


def _fox_mixer(x, bsz, t, g, w_in, b_f, gq, gk):
    n, d = x.shape
    hd = d // FOX_HEADS
    gq_s = gq * (LOG2E / math.sqrt(hd))
    qa, ka, va, sg, c = _fox_proj(x.reshape(bsz, t, d), g, w_in, b_f, gq_s, gk)
    bound = 1.02 * hd * jnp.max(jnp.abs(gq)) / math.sqrt(hd) * jnp.max(jnp.abs(gk))
    return _fox_attn(qa, ka, va, sg, c, bound).reshape(n, d)


def kernel(x, norm_g, ffn_w_in, ffn_w_out, hg_w_in, hg_lb_logits, hg_out_norm_g, hg_w_out,
           fox_w_in, fox_b_f, fox_q_norm_g, fox_k_norm_g, fox_w_out):
    bsz, t, d = x.shape
    depth = norm_g.shape[0]
    h = x.reshape(bsz * t, d)
    for layer in range(depth):
        j = layer // N_MIXERS
        h = _ffn(h, norm_g[layer, 0], ffn_w_in[layer, 0], ffn_w_out[layer, 0])
        if layer % N_MIXERS == 0:
            o = _hgrn_mixer(h, bsz, t, norm_g[layer, 1], hg_w_in[j], hg_lb_logits, j, hg_out_norm_g[j])
            w_o = hg_w_out[j]
        else:
            o = _fox_mixer(h, bsz, t, norm_g[layer, 1], fox_w_in[j], fox_b_f[j],
                           fox_q_norm_g[j], fox_k_norm_g[j])
            w_o = fox_w_out[j]
        h = _ffn(h, norm_g[layer, 2], ffn_w_in[layer, 1], ffn_w_out[layer, 1], mixer=(o, w_o))
    return h.reshape(bsz, t, d)
```

```python
import functools
import math

import jax
import jax.numpy as jnp
from jax import lax
from jax.experimental import pallas as pl
from jax.experimental.pallas import tpu as pltpu

F32 = jnp.float32
BF16 = jnp.bfloat16

EPS = 1e-6
MIN_GATE = 1e-30
MASK_VALUE = -1e30
HG_HEADS = 8
FOX_HEADS = 16
N_MIXERS = 2

LANES = 128
VMEM_LIMIT_BYTES = 56 * 1024 * 1024

_NT = (((1,), (1,)), ((), ()))
_TN = (((0,), (0,)), ((), ()))


def _params(*semantics):
    return pltpu.CompilerParams(dimension_semantics=semantics, vmem_limit_bytes=VMEM_LIMIT_BYTES)


def _rmsnorm(x, g):
    ms = jnp.mean(x * x, axis=-1, keepdims=True)
    return x * lax.rsqrt(ms + EPS) * g


def _silu(y):
    return y * jax.nn.sigmoid(y)


def _resident(shape):
    return pl.BlockSpec(shape, lambda *_: (0,) * len(shape), pipeline_mode=pl.Buffered(1))


def _ffn_body(*refs, d_ff, tf, has_mixer):
    if has_mixer:
        x_ref, mo_ref, mw_ref, g_ref, win_ref, wout_ref, o_ref, hn_ref, a_ref, xr_ref = refs
        xr_ref[...] = x_ref[...] + jnp.dot(mo_ref[...], mw_ref[...], preferred_element_type=F32)
        x_ref = xr_ref
    else:
        x_ref, g_ref, win_ref, wout_ref, o_ref, hn_ref, a_ref = refs
    hn_ref[...] = _rmsnorm(x_ref[...], g_ref[...]).astype(BF16)
    for f in range(d_ff // tf):
        hn = hn_ref[...]
        gate = jnp.dot(hn, win_ref[:, f * tf:(f + 1) * tf], preferred_element_type=F32)
        up = jnp.dot(hn, win_ref[:, d_ff + f * tf:d_ff + (f + 1) * tf], preferred_element_type=F32)
        a_ref[:, f * tf:(f + 1) * tf] = (_silu(gate) * up).astype(BF16)
    y = jnp.dot(a_ref[...], wout_ref[...], preferred_element_type=F32)
    o_ref[...] = x_ref[...] + 0.5 * y


def _ffn(x, g, w_in, w_out, mixer=None):
    n, d = x.shape
    d_ff = w_out.shape[0]
    tm = min(512, n)
    tf = 256
    row = pl.BlockSpec((tm, d), lambda i: (i, 0))
    args, specs, scratch = [x], [row], [pltpu.VMEM((tm, d), BF16), pltpu.VMEM((tm, d_ff), BF16)]
    if mixer is not None:
        args += [mixer[0], mixer[1].astype(BF16)]
        specs += [row, _resident((d, d))]
        scratch.append(pltpu.VMEM((tm, d), F32))
    return pl.pallas_call(
        functools.partial(_ffn_body, d_ff=d_ff, tf=tf, has_mixer=mixer is not None),
        grid=(n // tm,),
        in_specs=specs + [_resident((1, d)), _resident((d, 2 * d_ff)), _resident((d_ff, d))],
        out_specs=row,
        out_shape=jax.ShapeDtypeStruct((n, d), F32),
        scratch_shapes=scratch,
        compiler_params=_params("parallel"),
        name="ffn",
    )(*args, g.reshape(1, d), w_in.astype(BF16), w_out.astype(BF16))


HG_CHUNK = 128
HG_BAND = 4
HG_GROUP = 8
SUBLANES = 8


def _hg_body(x_ref, g_ref, w_ref, lbl_ref, gn_ref, o_ref, hn_ref, ya_ref, yb_ref, st_ref, *, layer_j, tb):
    c_len = HG_CHUNK
    dk = o_ref.shape[-1]
    t_blk, h = pl.program_id(1), pl.program_id(2)
    n_heads = pl.num_programs(2)

    @pl.when(h == 0)
    def _():
        hn_ref[...] = _rmsnorm(x_ref[0], g_ref[...]).astype(BF16)
        ya_ref[...] = jnp.dot(hn_ref[...], w_ref[0], preferred_element_type=F32)

    @pl.when(t_blk == 0)
    def _():
        st_ref[h] = jnp.zeros(st_ref.shape[1:], F32)

    n_layers = lbl_ref.shape[0]
    rows = [lbl_ref[i, 0] for i in range(n_layers)]
    mx = functools.reduce(jnp.maximum, rows)
    es = [jnp.exp(r - mx) for r in rows]
    den = functools.reduce(jnp.add, es)
    ps = [e / den for e in es]
    lb = functools.reduce(jnp.add, ps[:layer_j + 1]) - ps[0]
    next_head = jnp.minimum(h + 1, n_heads - 1)

    row = lax.broadcasted_iota(jnp.int32, (c_len, 1), 0)
    rr = lax.broadcasted_iota(jnp.int32, (c_len, c_len), 0)
    cc = lax.broadcasted_iota(jnp.int32, (c_len, c_len), 1)
    tri = (cc <= rr).astype(BF16)
    levels = []
    m = c_len // 2
    while m >= HG_BAND:
        levels.append(m)
        m //= 2
    diff = jnp.where(cc < rr, rr ^ cc, 0)
    pair_level = functools.reduce(jnp.add, [(diff >= m).astype(jnp.int32) for m in levels])
    band_ok = [(row % HG_BAND) >= dd for dd in range(HG_BAND)]

    def roll_in_tile(a, dd):
        a3 = a.reshape(c_len // SUBLANES, SUBLANES, dk)
        return pltpu.roll(a3, dd, 1).reshape(c_len, dk)

    def chunk(y_ref, r0):
        r0 = pl.multiple_of(r0, c_len)
        y = y_ref[pl.ds(r0, c_len), :]
        q = _silu(y[:, :dk])
        z = y[:, dk:2 * dk]
        v = y[:, 2 * dk:3 * dk]
        gs = _silu(y[:, 3 * dk:])
        lf = jnp.log(jnp.maximum(lb + (1.0 - lb) * jax.nn.sigmoid(z), MIN_GATE))
        k = (1.0 - lb) * jax.nn.sigmoid(-z)
        lf_hi = lf.astype(BF16)
        lf_lo = (lf - lf_hi.astype(F32)).astype(BF16)
        b = (jnp.dot(tri, lf_hi, preferred_element_type=F32)
             + jnp.dot(tri, lf_lo, preferred_element_type=F32))
        b_last = b[c_len - 1:c_len, :]
        st = st_ref[h]
        o = lax.dot_general((q * jnp.exp(b)).astype(BF16), st.astype(BF16), _NT,
                            preferred_element_type=F32)
        sc = jnp.zeros((c_len, c_len), F32)
        for idx, m in enumerate(levels):
            b3 = b.reshape(c_len // (2 * m), 2 * m, dk)
            ref = jnp.broadcast_to(b3[:, m:m + 1, :], b3.shape).reshape(c_len, dk)
            e = jnp.exp(-jnp.abs(b - ref))
            s_m = lax.dot_general((q * e).astype(BF16), (k * e).astype(BF16), _NT,
                                  preferred_element_type=F32)
            sc = jnp.where(pair_level == len(levels) - idx, s_m, sc)
        o = o + jnp.dot(sc.astype(BF16), v.astype(BF16), preferred_element_type=F32)
        o = o + jnp.sum(q * k, axis=-1, keepdims=True) * v
        for dd in range(1, HG_BAND):
            kd = roll_in_tile(k, dd)
            bd = roll_in_tile(b, dd)
            vd = roll_in_tile(v, dd)
            w = jnp.exp(jnp.minimum(b - bd, 0.0))
            s_d = jnp.sum(q * kd * w, axis=-1, keepdims=True)
            o = o + jnp.where(band_ok[dd], s_d, 0.0) * vd
        k_dec = (k * jnp.exp(b_last - b)).astype(BF16)
        st_ref[h] = st * jnp.exp(b_last) + lax.dot_general(v.astype(BF16), k_dec, _TN,
                                                           preferred_element_type=F32)
        ms = jnp.mean(o * o, axis=-1, keepdims=True)
        o_ref[0, pl.ds(r0, c_len), :] = (o * lax.rsqrt(ms + EPS) * gn_ref[...] * gs).astype(o_ref.dtype)

    group_rows = HG_GROUP * c_len

    def run(cur_ref, nxt_ref):
        def group(gi, carry):
            g0 = pl.multiple_of(gi * group_rows, group_rows)
            nxt_ref[pl.ds(g0, group_rows), :] = jnp.dot(hn_ref[pl.ds(g0, group_rows), :], w_ref[next_head],
                                                        preferred_element_type=F32)
            for ci in range(HG_GROUP):
                chunk(cur_ref, g0 + ci * c_len)
            return carry

        lax.fori_loop(0, tb // group_rows, group, 0)

    @pl.when(h % 2 == 0)
    def _():
        run(ya_ref, yb_ref)

    @pl.when(h % 2 == 1)
    def _():
        run(yb_ref, ya_ref)


def _hgrn_mixer(x, bsz, t, g, w_in, lb_logits, layer_j, out_norm_g):
    n, d = x.shape
    nh = HG_HEADS
    dk = d // nh
    tb = min(2048, t)
    assert tb % (HG_GROUP * HG_CHUNK) == 0 and nh % 2 == 0
    w_heads = w_in.reshape(d, 4, nh, dk).transpose(2, 0, 1, 3).reshape(nh, d, 4 * dk).astype(BF16)
    n_layers = lb_logits.shape[0]
    o = pl.pallas_call(
        functools.partial(_hg_body, layer_j=layer_j, tb=tb),
        grid=(bsz, t // tb, nh),
        in_specs=[pl.BlockSpec((1, tb, d), lambda b, i, h: (b, i, 0)),
                  _resident((1, d)),
                  _resident((nh, d, 4 * dk)),
                  pl.BlockSpec((n_layers, 1, 1, dk), lambda b, i, h: (0, h, 0, 0)),
                  _resident((1, dk))],
        out_specs=pl.BlockSpec((1, tb, dk), lambda b, i, h: (b, i, h)),
        out_shape=jax.ShapeDtypeStruct((bsz, t, d), BF16),
        scratch_shapes=[pltpu.VMEM((tb, d), BF16), pltpu.VMEM((tb, 4 * dk), F32),
                        pltpu.VMEM((tb, 4 * dk), F32), pltpu.VMEM((nh, dk, dk), F32)],
        compiler_params=_params("parallel", "arbitrary", "arbitrary"),
        name="hgrn",
    )(x.reshape(bsz, t, d), g.reshape(1, d), w_heads, lb_logits.reshape(n_layers, nh, 1, dk),
      out_norm_g.reshape(1, dk))
    return o.reshape(n, d)


def _head_rmsnorm(y, g2, ones2, hd):
    sq = y * y
    hi = sq.astype(BF16)
    lo = (sq - hi.astype(F32)).astype(BF16)
    ss = jnp.dot(hi, ones2, preferred_element_type=F32) + jnp.dot(lo, ones2, preferred_element_type=F32)
    return y * lax.rsqrt(ss * (1.0 / hd) + EPS) * g2


def _split3(x):
    hi = x.astype(BF16).astype(F32)
    r1 = x - hi
    mid = r1.astype(BF16).astype(F32)
    lo = (r1 - mid).astype(BF16).astype(F32)
    return hi, mid, lo


FOX_AUG_C = 0
FOX_AUG_ONE = 3
FOX_AUG_REF = 16
FOX_PIECE_STRIDE = 16
FOX_ONE_LANE = 48
FOX_V_ROWS = 80


def _fox_placement(nh, hd):
    import numpy as np
    hpb = LANES // hd
    assert hpb == 2 and 3 * FOX_PIECE_STRIDE <= FOX_ONE_LANE and nh <= FOX_PIECE_STRIDE
    pk = np.zeros((nh // hpb, LANES, hpb * LANES), np.float32)
    pq = np.zeros((nh // hpb, hpb * LANES, LANES), np.float32)
    for p in range(nh // hpb):
        for e in range(hpb):
            h = p * hpb + e
            base = e * LANES + (hd if e == 0 else 0)
            for i in range(3):
                pk[p, FOX_PIECE_STRIDE * i + h, base + FOX_AUG_ONE + i] = -1.0
                pk[p, FOX_ONE_LANE, base + FOX_AUG_C + i] = 1.0
                pk[p, FOX_ONE_LANE, base + FOX_AUG_REF + i] = 1.0
                pq[p, base + FOX_AUG_C + i, FOX_PIECE_STRIDE * i + h] = 1.0
                pq[p, base + FOX_AUG_ONE + i, FOX_ONE_LANE] = 1.0
    return jnp.asarray(pk, BF16), jnp.asarray(pq, BF16)


def _fox_proj_body(x_ref, g_ref, w_ref, wf_ref, bf_ref, gq_ref, gk_ref, pk_ref, pq_ref,
                   qa_ref, ka_ref, va_ref, sg_ref, c_ref, hn_ref, carry_ref, *, hd, nh):
    tm, d = x_ref.shape[1], x_ref.shape[2]
    hn_ref[...] = _rmsnorm(x_ref[0], g_ref[...]).astype(BF16)

    @pl.when(pl.program_id(1) == 0)
    def _():
        carry_ref[...] = jnp.zeros_like(carry_ref)

    z = jnp.dot(hn_ref[...], wf_ref[...], preferred_element_type=F32) + bf_ref[...]
    lf = jnp.minimum(z, 0.0) - jnp.log1p(jnp.exp(-jnp.abs(z)))
    tri = (lax.broadcasted_iota(jnp.int32, (tm, tm), 1)
           <= lax.broadcasted_iota(jnp.int32, (tm, tm), 0)).astype(BF16)
    c = carry_ref[...]
    for piece in _split3(lf):
        c = c + jnp.dot(tri, piece.astype(BF16), preferred_element_type=F32)
    carry_ref[...] = c[tm - 1:tm, :]
    c_ref[0] = c[:, :nh]
    hi, mid, lo = _split3(c * LOG2E)
    lane = lax.broadcasted_iota(jnp.int32, (tm, LANES), 1)
    pieces = jnp.where(lane < FOX_PIECE_STRIDE, hi,
                       jnp.where(lane < 2 * FOX_PIECE_STRIDE, mid,
                                 jnp.where(lane < 3 * FOX_PIECE_STRIDE, lo,
                                           jnp.where(lane == FOX_ONE_LANE, 1.0, 0.0))))
    pieces_t = pieces.T.astype(BF16)
    pieces = pieces.astype(BF16)

    ri = lax.broadcasted_iota(jnp.int32, (LANES, LANES), 0)
    ci = lax.broadcasted_iota(jnp.int32, (LANES, LANES), 1)
    ones2 = ((ri // hd) == (ci // hd)).astype(BF16)
    ones_rows = jnp.ones((FOX_V_ROWS - hd, tm), F32)
    hpb = LANES // hd
    wide = 2 * LANES
    for part in range(4):
        for cw in range(d // wide):
            yw = jnp.dot(hn_ref[...], w_ref[:, part * d + cw * wide:part * d + (cw + 1) * wide],
                         preferred_element_type=F32)
            if part == 3:
                sg_ref[0, :, cw * wide:(cw + 1) * wide] = jax.nn.sigmoid(yw)
                continue
            for half in range(wide // LANES):
                p = cw * (wide // LANES) + half
                y = yw[:, half * LANES:(half + 1) * LANES]
                if part == 1:
                    kn = _head_rmsnorm(y, gk_ref[...], ones2, hd)
                    aug = jnp.dot(pieces, pk_ref[p], preferred_element_type=F32)
                    for e in range(hpb):
                        ka_ref[0, p * hpb + e] = jnp.where((lane // hd) == e, kn,
                                                           aug[:, e * LANES:(e + 1) * LANES]).astype(BF16)
                    continue
                y_t = y.T
                if part == 2:
                    for e in range(hpb):
                        va_ref[0, p * hpb + e] = jnp.concatenate(
                            [y_t[e * hd:(e + 1) * hd], ones_rows], axis=0).astype(BF16)
                    continue
                aug_t = jnp.dot(pq_ref[p], pieces_t, preferred_element_type=F32)
                for e in range(hpb):
                    rows = y_t[e * hd:(e + 1) * hd]
                    ms = jnp.mean(rows * rows, axis=0, keepdims=True)
                    qn = rows * lax.rsqrt(ms + EPS) * gq_ref[...]
                    base = e * LANES + (hd if e == 0 else 0)
                    bias = aug_t[base:base + LANES - hd]
                    qa_ref[0, p * hpb + e] = jnp.concatenate([qn, bias] if e == 0 else [bias, qn],
                                                             axis=0).astype(BF16)


def _fox_proj(x, g, w_in, b_f, gq, gk):
    bsz, t, d = x.shape
    nh = FOX_HEADS
    hd = d // nh
    tm = min(512, t)
    pk, pq = _fox_placement(nh, hd)
    reps = FOX_ONE_LANE // FOX_PIECE_STRIDE
    w_f = jnp.pad(jnp.tile(jnp.pad(w_in[:, 4 * d:], ((0, 0), (0, FOX_PIECE_STRIDE - nh))), (1, reps)),
                  ((0, 0), (0, LANES - FOX_ONE_LANE)))
    b_pad = jnp.pad(jnp.tile(jnp.pad(b_f, (0, FOX_PIECE_STRIDE - nh)), reps), (0, LANES - FOX_ONE_LANE))
    row = pl.BlockSpec((1, tm, d), lambda b, i: (b, i, 0))
    return pl.pallas_call(
        functools.partial(_fox_proj_body, hd=hd, nh=nh),
        grid=(bsz, t // tm),
        in_specs=[row, _resident((1, d)), _resident((d, 4 * d)), _resident((d, LANES)),
                  _resident((1, LANES)), _resident((hd, 1)), _resident((1, LANES)),
                  _resident(pk.shape), _resident(pq.shape)],
        out_specs=[pl.BlockSpec((1, nh, LANES, tm), lambda b, i: (b, 0, 0, i)),
                   pl.BlockSpec((1, nh, tm, LANES), lambda b, i: (b, 0, i, 0)),
                   pl.BlockSpec((1, nh, FOX_V_ROWS, tm), lambda b, i: (b, 0, 0, i)),
                   row,
                   pl.BlockSpec((1, tm, nh), lambda b, i: (b, i, 0))],
        out_shape=[jax.ShapeDtypeStruct((bsz, nh, LANES, t), BF16),
                   jax.ShapeDtypeStruct((bsz, nh, t, LANES), BF16),
                   jax.ShapeDtypeStruct((bsz, nh, FOX_V_ROWS, t), BF16),
                   jax.ShapeDtypeStruct((bsz, t, d), F32),
                   jax.ShapeDtypeStruct((bsz, t, nh), F32)],
        scratch_shapes=[pltpu.VMEM((tm, d), BF16), pltpu.VMEM((1, LANES), F32)],
        compiler_params=_params("parallel", "arbitrary"),
        name="fox_proj",
    )(x, g.reshape(1, d), w_in[:, :4 * d].astype(BF16), w_f.astype(BF16), b_pad.reshape(1, LANES),
      gq.reshape(hd, 1), jnp.tile(gk, LANES // hd).reshape(1, LANES), pk, pq)


LOG2E = 1.4426950408889634
FOX_TQ = 512
FOX_TK = 256
FOX_SKIP_LOG = 105.0
FOX_FIXED_REF_MAX = 60.0


def _fox_blocks(cs_ref, ce_ref, bound_ref, qa_ref, ka_ref, e, *, tq, tk, hd, nh, ref_rows=None):
    b, pr, i = pl.program_id(0), pl.program_id(1), pl.program_id(2)
    spb = tq // tk
    nq = pl.num_programs(2)
    head = pr * (LANES // hd) + e
    qa = qa_ref[0, e]
    if ref_rows is not None:
        r0 = (hd if e == 0 else 0) + FOX_AUG_REF
        qa = jnp.concatenate([qa[:r0], ref_rows, qa[r0 + ref_rows.shape[0]:]], axis=0)
    c_start = cs_ref[(b * nq + i) * nh + head]
    two_b = 2.0 * bound_ref[0]

    def keep_going(j):
        prev = jnp.maximum(j - 1, 0)
        return (j > 0) & (c_start - ce_ref[(b * nq * spb + prev) * nh + head] + two_b >= -FOX_SKIP_LOG)

    lo = lax.while_loop(keep_going, lambda j: j - 1, i * spb)
    rr = lax.broadcasted_iota(jnp.int32, (tk, tq), 0)
    cc = lax.broadcasted_iota(jnp.int32, (tk, tq), 1)

    def scores(j, masked):
        r0 = pl.multiple_of(j * tk, tk)
        s = jnp.dot(ka_ref[0, e, pl.ds(r0, tk), :], qa, preferred_element_type=F32)
        if masked:
            s = jnp.where(r0 + rr <= i * tq + cc, s, MASK_VALUE)
        return s

    return lo, i * spb, spb, scores


FOX_GROUP = 4


def _fox_attn_body(cs_ref, ce_ref, bound_ref, qa_ref, mr_ref, ka_ref, va_ref, sg_ref, o_ref,
                   qop_ref, s_ref, *acc_refs, tq, tk, hd, nh):
    i = pl.program_id(2)
    hpb = LANES // hd
    spb = tq // tk
    n_key_blocks = pl.num_programs(2) * spb
    grp = len(acc_refs)
    los = []
    for e in range(hpb):
        lo, diag, _, _ = _fox_blocks(cs_ref, ce_ref, bound_ref, qa_ref, ka_ref, e, tq=tq, tk=tk, hd=hd, nh=nh)
        los.append(lo)
        r0 = (hd if e == 0 else 0) + FOX_AUG_REF
        qa = qa_ref[0, e]
        qop_ref[e] = jnp.concatenate([qa[:r0], mr_ref[0, e], qa[r0 + mr_ref.shape[2]:]], axis=0)
    counts = [diag + spb - lo for lo in los]
    n_items = functools.reduce(jnp.add, counts)
    for acc_ref in acc_refs:
        acc_ref[...] = jnp.zeros_like(acc_ref)
    key_minus_query = (lax.broadcasted_iota(jnp.int32, (tk, tq), 0)
                       - lax.broadcasted_iota(jnp.int32, (tk, tq), 1))

    def item(w):
        e, j = jnp.int32(0), los[0] + w
        first = jnp.int32(0)
        for h in range(1, hpb):
            first = first + counts[h - 1]
            e = jnp.where(w >= first, h, e)
            j = jnp.where(w >= first, los[h] + (w - first), j)
        return e, j, jnp.minimum(j, n_key_blocks - 1), w < n_items

    def build_scores(w, slot):
        e, _, jc, _ = item(w)
        kb = ka_ref[e, pl.ds(pl.multiple_of(jc * tk, tk), tk), :]
        s_ref[slot] = jnp.dot(kb, qop_ref[e], preferred_element_type=F32)

    def consume(w, slot, acc_ref):
        e, j, jc, real = item(w)
        limit = jnp.where(real, i * tq - j * tk, -tq)
        p = jnp.exp2(jnp.where(key_minus_query <= limit, s_ref[slot], MASK_VALUE)).astype(BF16)
        vb = va_ref[e, :, pl.ds(pl.multiple_of(jc * tk, tk), tk)]
        acc_ref[e] += jnp.dot(vb, p, preferred_element_type=F32)

    ka_ref, va_ref = ka_ref.at[0], va_ref.at[0]
    for u in range(grp):
        build_scores(u, u)

    def stage(g, carry):
        for u in range(grp):
            consume(g * grp + u, u, acc_refs[u])
            build_scores((g + 1) * grp + u, u)
        return carry

    lax.fori_loop(0, (n_items + grp - 1) // grp, stage, 0)
    outs = []
    for e in range(hpb):
        acc = functools.reduce(jnp.add, [acc_ref[e] for acc_ref in acc_refs])
        outs.append(acc[:hd, :] / acc[hd:hd + 1, :])
    o = jnp.concatenate(outs, axis=0).T
    o_ref[0] = (o * sg_ref[0]).astype(o_ref.dtype)


def _fox_rowmax_body(cs_ref, ce_ref, bound_ref, qa_ref, ka_ref, m_ref, *, tq, tk, hd, nh):
    for e in range(LANES // hd):
        lo, diag, spb, scores = _fox_blocks(cs_ref, ce_ref, bound_ref, qa_ref, ka_ref, e,
                                            tq=tq, tk=tk, hd=hd, nh=nh)
        m = lax.fori_loop(lo, diag,
                          lambda j, m: jnp.maximum(m, jnp.max(scores(j, False), axis=0, keepdims=True)),
                          jnp.full((1, tq), MASK_VALUE, F32))
        for sb in range(spb):
            m = jnp.maximum(m, jnp.max(scores(diag + sb, True), axis=0, keepdims=True))
        m_ref[0, e] = jnp.broadcast_to(m, m_ref.shape[2:])


def _fox_attn(qa, ka, va, sg, c, bound):
    bsz, nh, _, t = qa.shape
    d = sg.shape[-1]
    hd = d // nh
    hpb = LANES // hd
    tq, tk = min(FOX_TQ, t), min(FOX_TK, t)
    c_start = c[:, 0::tq, :].reshape(-1)
    c_end = c[:, tk - 1::tk, :].reshape(-1)
    smem = pl.BlockSpec(memory_space=pltpu.SMEM)
    grid = (bsz, nh // hpb, t // tq)
    qblk = pl.BlockSpec((1, hpb, LANES, tq), lambda b, p, i: (b, p, 0, i))
    kblk = pl.BlockSpec((1, hpb, t, LANES), lambda b, p, i: (b, p, 0, 0))
    static = dict(tq=tq, tk=tk, hd=hd, nh=nh)
    scalars = (c_start, c_end, bound.reshape(1))
    bound2 = bound * LOG2E

    def true_rowmax():
        m = pl.pallas_call(
            functools.partial(_fox_rowmax_body, **static),
            grid=grid,
            in_specs=[smem, smem, smem, qblk, kblk],
            out_specs=pl.BlockSpec((1, hpb, 8, tq), lambda b, p, i: (b, p, 0, i)),
            out_shape=jax.ShapeDtypeStruct((bsz, nh, 8, t), F32),
            compiler_params=_params("parallel", "parallel", "arbitrary"),
            name="fox_rowmax",
        )(*scalars, qa, ka)
        return m[:, :, 0, :]

    ref = lax.cond(2.0 * bound2 <= FOX_FIXED_REF_MAX,
                   lambda: jnp.full((bsz, nh, t), bound2, F32), true_rowmax)
    pieces = jnp.stack([-p for p in _split3(ref)], axis=2)
    mr = jnp.pad(pieces, ((0, 0), (0, 0), (0, 16 - 3), (0, 0))).astype(BF16)
    oblk = pl.BlockSpec((1, tq, LANES), lambda b, p, i: (b, i, p))
    return pl.pallas_call(
        functools.partial(_fox_attn_body, **static),
        grid=grid,
        in_specs=[smem, smem, smem, qblk,
                  pl.BlockSpec((1, hpb, 16, tq), lambda b, p, i: (b, p, 0, i)),
                  kblk,
                  pl.BlockSpec((1, hpb, FOX_V_ROWS, t), lambda b, p, i: (b, p, 0, 0)),
                  oblk],
        out_specs=oblk,
        out_shape=jax.ShapeDtypeStruct((bsz, t, d), BF16),
        scratch_shapes=[pltpu.VMEM((hpb, LANES, tq), BF16), pltpu.VMEM((FOX_GROUP, tk, tq), F32)]
        + [pltpu.VMEM((hpb, FOX_V_ROWS, tq), F32)] * FOX_GROUP,
        compiler_params=_params("parallel", "parallel", "arbitrary"),
        name="fox_attn",
    )(*scalars, qa, mr, ka, va, sg)


def _fox_mixer(x, bsz, t, g, w_in, b_f, gq, gk):
    n, d = x.shape
    hd = d // FOX_HEADS
    gq_s = gq * (LOG2E / math.sqrt(hd))
    qa, ka, va, sg, c = _fox_proj(x.reshape(bsz, t, d), g, w_in, b_f, gq_s, gk)
    bound = 1.02 * hd * jnp.max(jnp.abs(gq)) / math.sqrt(hd) * jnp.max(jnp.abs(gk))
    return _fox_attn(qa, ka, va, sg, c, bound).reshape(n, d)


def kernel(x, norm_g, ffn_w_in, ffn_w_out, hg_w_in, hg_lb_logits, hg_out_norm_g, hg_w_out,
           fox_w_in, fox_b_f, fox_q_norm_g, fox_k_norm_g, fox_w_out):
    bsz, t, d = x.shape
    depth = norm_g.shape[0]
    h = x.reshape(bsz * t, d)
    for layer in range(depth):
        j = layer // N_MIXERS
        h = _ffn(h, norm_g[layer, 0], ffn_w_in[layer, 0], ffn_w_out[layer, 0])
        if layer % N_MIXERS == 0:
            o = _hgrn_mixer(h, bsz, t, norm_g[layer, 1], hg_w_in[j], hg_lb_logits, j, hg_out_norm_g[j])
            w_o = hg_w_out[j]
        else:
            o = _fox_mixer(h, bsz, t, norm_g[layer, 1], fox_w_in[j], fox_b_f[j],
                           fox_q_norm_g[j], fox_k_norm_g[j])
            w_o = fox_w_out[j]
        h = _ffn(h, norm_g[layer, 2], ffn_w_in[layer, 1], ffn_w_out[layer, 1], mixer=(o, w_o))
    return h.reshape(bsz, t, d)
```

```python
import functools
import math

import jax
import jax.numpy as jnp
from jax import lax
from jax.experimental import pallas as pl
from jax.experimental.pallas import tpu as pltpu

F32 = jnp.float32
BF16 = jnp.bfloat16

EPS = 1e-6
MIN_GATE = 1e-30
MASK_VALUE = -1e30
HG_HEADS = 8
FOX_HEADS = 16
N_MIXERS = 2

LANES = 128
VMEM_LIMIT_BYTES = 56 * 1024 * 1024

_NT = (((1,), (1,)), ((), ()))
_TN = (((0,), (0,)), ((), ()))


def _params(*semantics):
    return pltpu.CompilerParams(dimension_semantics=semantics, vmem_limit_bytes=VMEM_LIMIT_BYTES)


def _rmsnorm(x, g):
    ms = jnp.mean(x * x, axis=-1, keepdims=True)
    return x * lax.rsqrt(ms + EPS) * g


_sigmoid = jax.nn.sigmoid


def _silu(y):
    return y * _sigmoid(y)


def _resident(shape):
    return pl.BlockSpec(shape, lambda *_: (0,) * len(shape), pipeline_mode=pl.Buffered(1))


def _ffn_body(*refs, d_ff, tf, has_mixer):
    if has_mixer:
        x_ref, mo_ref, mw_ref, g_ref, win_ref, wout_ref, o_ref, hn_ref, a_ref, xr_ref = refs
        xr_ref[...] = x_ref[...] + jnp.dot(mo_ref[...], mw_ref[...], preferred_element_type=F32)
        x_ref = xr_ref
    else:
        x_ref, g_ref, win_ref, wout_ref, o_ref, hn_ref, a_ref = refs
    hn_ref[...] = _rmsnorm(x_ref[...], g_ref[...]).astype(BF16)
    for f in range(d_ff // tf):
        hn = hn_ref[...]
        gate = jnp.dot(hn, win_ref[:, f * tf:(f + 1) * tf], preferred_element_type=F32)
        up = jnp.dot(hn, win_ref[:, d_ff + f * tf:d_ff + (f + 1) * tf], preferred_element_type=F32)
        a_ref[:, f * tf:(f + 1) * tf] = (_silu(gate) * up).astype(BF16)
    y = jnp.dot(a_ref[...], wout_ref[...], preferred_element_type=F32)
    o_ref[...] = x_ref[...] + 0.5 * y


def _ffn(x, g, w_in, w_out, mixer=None):
    n, d = x.shape
    d_ff = w_out.shape[0]
    tm = min(512, n)
    tf = 256
    row = pl.BlockSpec((tm, d), lambda i: (i, 0))
    args, specs, scratch = [x], [row], [pltpu.VMEM((tm, d), BF16), pltpu.VMEM((tm, d_ff), BF16)]
    if mixer is not None:
        args += [mixer[0], mixer[1].astype(BF16)]
        specs += [row, _resident((d, d))]
        scratch.append(pltpu.VMEM((tm, d), F32))
    return pl.pallas_call(
        functools.partial(_ffn_body, d_ff=d_ff, tf=tf, has_mixer=mixer is not None),
        grid=(n // tm,),
        in_specs=specs + [_resident((1, d)), _resident((d, 2 * d_ff)), _resident((d_ff, d))],
        out_specs=row,
        out_shape=jax.ShapeDtypeStruct((n, d), F32),
        scratch_shapes=scratch,
        compiler_params=_params("parallel"),
        name="ffn",
    )(*args, g.reshape(1, d), w_in.astype(BF16), w_out.astype(BF16))


HG_CHUNK = 128
HG_BAND = 4
HG_GROUP = 8
SUBLANES = 8


def _hg_body(x_ref, g_ref, w_ref, lbl_ref, gn_ref, o_ref, hn_ref, ya_ref, yb_ref, st_ref, *, layer_j, tb):
    c_len = HG_CHUNK
    dk = o_ref.shape[-1]
    t_blk, h = pl.program_id(1), pl.program_id(2)
    n_heads = pl.num_programs(2)

    @pl.when(h == 0)
    def _():
        hn_ref[...] = _rmsnorm(x_ref[0], g_ref[...]).astype(BF16)
        ya_ref[...] = jnp.dot(hn_ref[...], w_ref[0], preferred_element_type=F32)

    @pl.when(t_blk == 0)
    def _():
        st_ref[h] = jnp.zeros(st_ref.shape[1:], F32)

    n_layers = lbl_ref.shape[0]
    rows = [lbl_ref[i, 0] for i in range(n_layers)]
    mx = functools.reduce(jnp.maximum, rows)
    es = [jnp.exp(r - mx) for r in rows]
    den = functools.reduce(jnp.add, es)
    ps = [e / den for e in es]
    lb = functools.reduce(jnp.add, ps[:layer_j + 1]) - ps[0]
    next_head = jnp.minimum(h + 1, n_heads - 1)

    row = lax.broadcasted_iota(jnp.int32, (c_len, 1), 0)
    rr = lax.broadcasted_iota(jnp.int32, (c_len, c_len), 0)
    cc = lax.broadcasted_iota(jnp.int32, (c_len, c_len), 1)
    tri = (cc <= rr).astype(BF16)
    levels = []
    m = c_len // 2
    while m >= HG_BAND:
        levels.append(m)
        m //= 2
    diff = jnp.where(cc < rr, rr ^ cc, 0)
    pair_level = functools.reduce(jnp.add, [(diff >= m).astype(jnp.int32) for m in levels])
    band_ok = [(row % HG_BAND) >= dd for dd in range(HG_BAND)]

    def roll_in_tile(a, dd):
        a3 = a.reshape(c_len // SUBLANES, SUBLANES, dk)
        return pltpu.roll(a3, dd, 1).reshape(c_len, dk)

    def chunk(y_ref, r0):
        r0 = pl.multiple_of(r0, c_len)
        y = y_ref[pl.ds(r0, c_len), :]
        q = _silu(y[:, :dk])
        z = y[:, dk:2 * dk]
        v = y[:, 2 * dk:3 * dk]
        gs = _silu(y[:, 3 * dk:])
        lf = jnp.log(jnp.maximum(lb + (1.0 - lb) * _sigmoid(z), MIN_GATE))
        k = (1.0 - lb) * _sigmoid(-z)
        lf_hi = lf.astype(BF16)
        lf_lo = (lf - lf_hi.astype(F32)).astype(BF16)
        b = (jnp.dot(tri, lf_hi, preferred_element_type=F32)
             + jnp.dot(tri, lf_lo, preferred_element_type=F32))
        b_last = b[c_len - 1:c_len, :]
        st = st_ref[h]
        o = lax.dot_general((q * jnp.exp(b)).astype(BF16), st.astype(BF16), _NT,
                            preferred_element_type=F32)
        sc = jnp.zeros((c_len, c_len), F32)
        for idx, m in enumerate(levels):
            b3 = b.reshape(c_len // (2 * m), 2 * m, dk)
            ref = jnp.broadcast_to(b3[:, m:m + 1, :], b3.shape).reshape(c_len, dk)
            e = jnp.exp(-jnp.abs(b - ref))
            s_m = lax.dot_general((q * e).astype(BF16), (k * e).astype(BF16), _NT,
                                  preferred_element_type=F32)
            sc = jnp.where(pair_level == len(levels) - idx, s_m, sc)
        o = o + jnp.dot(sc.astype(BF16), v.astype(BF16), preferred_element_type=F32)
        o = o + jnp.sum(q * k, axis=-1, keepdims=True) * v
        for dd in range(1, HG_BAND):
            kd = roll_in_tile(k, dd)
            bd = roll_in_tile(b, dd)
            vd = roll_in_tile(v, dd)
            w = jnp.exp(jnp.minimum(b - bd, 0.0))
            s_d = jnp.sum(q * kd * w, axis=-1, keepdims=True)
            o = o + jnp.where(band_ok[dd], s_d, 0.0) * vd
        k_dec = (k * jnp.exp(b_last - b)).astype(BF16)
        st_ref[h] = st * jnp.exp(b_last) + lax.dot_general(v.astype(BF16), k_dec, _TN,
                                                           preferred_element_type=F32)
        ms = jnp.mean(o * o, axis=-1, keepdims=True)
        o_ref[0, pl.ds(r0, c_len), :] = (o * lax.rsqrt(ms + EPS) * gn_ref[...] * gs).astype(o_ref.dtype)

    group_rows = HG_GROUP * c_len

    def run(cur_ref, nxt_ref):
        def group(gi, carry):
            g0 = pl.multiple_of(gi * group_rows, group_rows)
            nxt_ref[pl.ds(g0, group_rows), :] = jnp.dot(hn_ref[pl.ds(g0, group_rows), :], w_ref[next_head],
                                                        preferred_element_type=F32)
            for ci in range(HG_GROUP):
                chunk(cur_ref, g0 + ci * c_len)
            return carry

        lax.fori_loop(0, tb // group_rows, group, 0)

    @pl.when(h % 2 == 0)
    def _():
        run(ya_ref, yb_ref)

    @pl.when(h % 2 == 1)
    def _():
        run(yb_ref, ya_ref)


def _hgrn_mixer(x, bsz, t, g, w_in, lb_logits, layer_j, out_norm_g):
    n, d = x.shape
    nh = HG_HEADS
    dk = d // nh
    tb = min(2048, t)
    assert tb % (HG_GROUP * HG_CHUNK) == 0 and nh % 2 == 0
    w_heads = w_in.reshape(d, 4, nh, dk).transpose(2, 0, 1, 3).reshape(nh, d, 4 * dk).astype(BF16)
    n_layers = lb_logits.shape[0]
    o = pl.pallas_call(
        functools.partial(_hg_body, layer_j=layer_j, tb=tb),
        grid=(bsz, t // tb, nh),
        in_specs=[pl.BlockSpec((1, tb, d), lambda b, i, h: (b, i, 0)),
                  _resident((1, d)),
                  _resident((nh, d, 4 * dk)),
                  pl.BlockSpec((n_layers, 1, 1, dk), lambda b, i, h: (0, h, 0, 0)),
                  _resident((1, dk))],
        out_specs=pl.BlockSpec((1, tb, dk), lambda b, i, h: (b, i, h)),
        out_shape=jax.ShapeDtypeStruct((bsz, t, d), BF16),
        scratch_shapes=[pltpu.VMEM((tb, d), BF16), pltpu.VMEM((tb, 4 * dk), F32),
                        pltpu.VMEM((tb, 4 * dk), F32), pltpu.VMEM((nh, dk, dk), F32)],
        compiler_params=_params("parallel", "arbitrary", "arbitrary"),
        name="hgrn",
    )(x.reshape(bsz, t, d), g.reshape(1, d), w_heads, lb_logits.reshape(n_layers, nh, 1, dk),
      out_norm_g.reshape(1, dk))
    return o.reshape(n, d)


def _head_rmsnorm(y, g2, ones2, hd):
    sq = y * y
    hi = sq.astype(BF16)
    lo = (sq - hi.astype(F32)).astype(BF16)
    ss = jnp.dot(hi, ones2, preferred_element_type=F32) + jnp.dot(lo, ones2, preferred_element_type=F32)
    return y * lax.rsqrt(ss * (1.0 / hd) + EPS) * g2


def _split3(x):
    hi = x.astype(BF16).astype(F32)
    r1 = x - hi
    mid = r1.astype(BF16).astype(F32)
    lo = (r1 - mid).astype(BF16).astype(F32)
    return hi, mid, lo


FOX_AUG_C = 0
FOX_AUG_ONE = 3
FOX_AUG_REF = 16
FOX_PIECE_STRIDE = 16
FOX_ONE_LANE = 48
FOX_V_ROWS = 80


def _fox_placement(nh, hd):
    import numpy as np
    hpb = LANES // hd
    assert hpb == 2 and 3 * FOX_PIECE_STRIDE <= FOX_ONE_LANE and nh <= FOX_PIECE_STRIDE
    pk = np.zeros((nh // hpb, LANES, hpb * LANES), np.float32)
    pq = np.zeros((nh // hpb, hpb * LANES, LANES), np.float32)
    for p in range(nh // hpb):
        for e in range(hpb):
            h = p * hpb + e
            base = e * LANES + (hd if e == 0 else 0)
            for i in range(3):
                pk[p, FOX_PIECE_STRIDE * i + h, base + FOX_AUG_ONE + i] = -1.0
                pk[p, FOX_ONE_LANE, base + FOX_AUG_C + i] = 1.0
                pk[p, FOX_ONE_LANE, base + FOX_AUG_REF + i] = 1.0
                pq[p, base + FOX_AUG_C + i, FOX_PIECE_STRIDE * i + h] = 1.0
                pq[p, base + FOX_AUG_ONE + i, FOX_ONE_LANE] = 1.0
    return jnp.asarray(pk, BF16), jnp.asarray(pq, BF16)


def _fox_proj_body(x_ref, g_ref, w_ref, wf_ref, bf_ref, gq_ref, gk_ref, pk_ref, pq_ref,
                   qa_ref, ka_ref, va_ref, sg_ref, c_ref, hn_ref, carry_ref, *, hd, nh):
    tm, d = x_ref.shape[1], x_ref.shape[2]
    hn_ref[...] = _rmsnorm(x_ref[0], g_ref[...]).astype(BF16)

    @pl.when(pl.program_id(1) == 0)
    def _():
        carry_ref[...] = jnp.zeros_like(carry_ref)

    z = jnp.dot(hn_ref[...], wf_ref[...], preferred_element_type=F32) + bf_ref[...]
    lf = jnp.minimum(z, 0.0) - jnp.log1p(jnp.exp(-jnp.abs(z)))
    tri = (lax.broadcasted_iota(jnp.int32, (tm, tm), 1)
           <= lax.broadcasted_iota(jnp.int32, (tm, tm), 0)).astype(BF16)
    c = carry_ref[...]
    for piece in _split3(lf):
        c = c + jnp.dot(tri, piece.astype(BF16), preferred_element_type=F32)
    carry_ref[...] = c[tm - 1:tm, :]
    c_ref[0] = c[:, :nh]
    hi, mid, lo = _split3(c * LOG2E)
    lane = lax.broadcasted_iota(jnp.int32, (tm, LANES), 1)
    pieces = jnp.where(lane < FOX_PIECE_STRIDE, hi,
                       jnp.where(lane < 2 * FOX_PIECE_STRIDE, mid,
                                 jnp.where(lane < 3 * FOX_PIECE_STRIDE, lo,
                                           jnp.where(lane == FOX_ONE_LANE, 1.0, 0.0))))
    pieces_t = pieces.T.astype(BF16)
    pieces = pieces.astype(BF16)

    ri = lax.broadcasted_iota(jnp.int32, (LANES, LANES), 0)
    ci = lax.broadcasted_iota(jnp.int32, (LANES, LANES), 1)
    ones2 = ((ri // hd) == (ci // hd)).astype(BF16)
    ones_rows = jnp.ones((FOX_V_ROWS - hd, tm), F32)
    hpb = LANES // hd
    wide = 2 * LANES
    for part in range(4):
        for cw in range(d // wide):
            yw = jnp.dot(hn_ref[...], w_ref[:, part * d + cw * wide:part * d + (cw + 1) * wide],
                         preferred_element_type=F32)
            if part == 3:
                sg_ref[0, :, cw * wide:(cw + 1) * wide] = _sigmoid(yw)
                continue
            for half in range(wide // LANES):
                p = cw * (wide // LANES) + half
                y = yw[:, half * LANES:(half + 1) * LANES]
                if part == 1:
                    kn = _head_rmsnorm(y, gk_ref[...], ones2, hd)
                    aug = jnp.dot(pieces, pk_ref[p], preferred_element_type=F32)
                    for e in range(hpb):
                        ka_ref[0, p * hpb + e] = jnp.where((lane // hd) == e, kn,
                                                           aug[:, e * LANES:(e + 1) * LANES]).astype(BF16)
                    continue
                y_t = y.T
                if part == 2:
                    for e in range(hpb):
                        va_ref[0, p * hpb + e] = jnp.concatenate(
                            [y_t[e * hd:(e + 1) * hd], ones_rows], axis=0).astype(BF16)
                    continue
                aug_t = jnp.dot(pq_ref[p], pieces_t, preferred_element_type=F32)
                for e in range(hpb):
                    rows = y_t[e * hd:(e + 1) * hd]
                    ms = jnp.mean(rows * rows, axis=0, keepdims=True)
                    qn = rows * lax.rsqrt(ms + EPS) * gq_ref[...]
                    base = e * LANES + (hd if e == 0 else 0)
                    bias = aug_t[base:base + LANES - hd]
                    qa_ref[0, p * hpb + e] = jnp.concatenate([qn, bias] if e == 0 else [bias, qn],
                                                             axis=0).astype(BF16)


def _fox_proj(x, g, w_in, b_f, gq, gk):
    bsz, t, d = x.shape
    nh = FOX_HEADS
    hd = d // nh
    tm = min(512, t)
    pk, pq = _fox_placement(nh, hd)
    reps = FOX_ONE_LANE // FOX_PIECE_STRIDE
    w_f = jnp.pad(jnp.tile(jnp.pad(w_in[:, 4 * d:], ((0, 0), (0, FOX_PIECE_STRIDE - nh))), (1, reps)),
                  ((0, 0), (0, LANES - FOX_ONE_LANE)))
    b_pad = jnp.pad(jnp.tile(jnp.pad(b_f, (0, FOX_PIECE_STRIDE - nh)), reps), (0, LANES - FOX_ONE_LANE))
    row = pl.BlockSpec((1, tm, d), lambda b, i: (b, i, 0))
    return pl.pallas_call(
        functools.partial(_fox_proj_body, hd=hd, nh=nh),
        grid=(bsz, t // tm),
        in_specs=[row, _resident((1, d)), _resident((d, 4 * d)), _resident((d, LANES)),
                  _resident((1, LANES)), _resident((hd, 1)), _resident((1, LANES)),
                  _resident(pk.shape), _resident(pq.shape)],
        out_specs=[pl.BlockSpec((1, nh, LANES, tm), lambda b, i: (b, 0, 0, i)),
                   pl.BlockSpec((1, nh, tm, LANES), lambda b, i: (b, 0, i, 0)),
                   pl.BlockSpec((1, nh, FOX_V_ROWS, tm), lambda b, i: (b, 0, 0, i)),
                   row,
                   pl.BlockSpec((1, tm, nh), lambda b, i: (b, i, 0))],
        out_shape=[jax.ShapeDtypeStruct((bsz, nh, LANES, t), BF16),
                   jax.ShapeDtypeStruct((bsz, nh, t, LANES), BF16),
                   jax.ShapeDtypeStruct((bsz, nh, FOX_V_ROWS, t), BF16),
                   jax.ShapeDtypeStruct((bsz, t, d), F32),
                   jax.ShapeDtypeStruct((bsz, t, nh), F32)],
        scratch_shapes=[pltpu.VMEM((tm, d), BF16), pltpu.VMEM((1, LANES), F32)],
        compiler_params=_params("parallel", "arbitrary"),
        name="fox_proj",
    )(x, g.reshape(1, d), w_in[:, :4 * d].astype(BF16), w_f.astype(BF16), b_pad.reshape(1, LANES),
      gq.reshape(hd, 1), jnp.tile(gk, LANES // hd).reshape(1, LANES), pk, pq)


LOG2E = 1.4426950408889634
FOX_TQ = 512
FOX_TK = 256
FOX_SKIP_LOG = 105.0
FOX_FIXED_REF_MAX = 60.0


def _fox_blocks(lo_ref, qa_ref, ka_ref, e, *, tq, tk, hd, nh, ref_rows=None):
    b, pr, i = pl.program_id(0), pl.program_id(1), pl.program_id(2)
    spb = tq // tk
    head = pr * (LANES // hd) + e
    qa = qa_ref[0, e]
    if ref_rows is not None:
        r0 = (hd if e == 0 else 0) + FOX_AUG_REF
        qa = jnp.concatenate([qa[:r0], ref_rows, qa[r0 + ref_rows.shape[0]:]], axis=0)
    lo = lo_ref[(b * pl.num_programs(2) + i) * nh + head]
    rr = lax.broadcasted_iota(jnp.int32, (tk, tq), 0)
    cc = lax.broadcasted_iota(jnp.int32, (tk, tq), 1)

    def scores(j, masked):
        r0 = pl.multiple_of(j * tk, tk)
        s = jnp.dot(ka_ref[0, e, pl.ds(r0, tk), :], qa, preferred_element_type=F32)
        if masked:
            s = jnp.where(r0 + rr <= i * tq + cc, s, MASK_VALUE)
        return s

    return lo, i * spb, spb, scores


FOX_GROUP = 4


def _fox_attn_body(lo_ref, qa_ref, mr_ref, ka_ref, va_ref, sg_ref, o_ref,
                   qop_ref, s_ref, acc_ref, *, tq, tk, hd, nh):
    i = pl.program_id(2)
    hpb = LANES // hd
    spb = tq // tk
    n_key_blocks = pl.num_programs(2) * spb
    grp = s_ref.shape[0]
    los = []
    for e in range(hpb):
        lo, diag, _, _ = _fox_blocks(lo_ref, qa_ref, ka_ref, e, tq=tq, tk=tk, hd=hd, nh=nh)
        los.append(lo)
        r0 = (hd if e == 0 else 0) + FOX_AUG_REF
        qa = qa_ref[0, e]
        qop_ref[e] = jnp.concatenate([qa[:r0], mr_ref[0, e], qa[r0 + mr_ref.shape[2]:]], axis=0)
    counts = [diag + spb - lo for lo in los]
    n_items = functools.reduce(jnp.add, counts)
    acc_ref[...] = jnp.zeros_like(acc_ref)
    key_minus_query = (lax.broadcasted_iota(jnp.int32, (tk, tq), 0)
                       - lax.broadcasted_iota(jnp.int32, (tk, tq), 1))

    def item(w):
        e, j = jnp.int32(0), los[0] + w
        first = jnp.int32(0)
        for h in range(1, hpb):
            first = first + counts[h - 1]
            e = jnp.where(w >= first, h, e)
            j = jnp.where(w >= first, los[h] + (w - first), j)
        return e, j, jnp.minimum(j, n_key_blocks - 1), w < n_items

    def build_scores(w, slot):
        e, _, jc, _ = item(w)
        kb = ka_ref[e, pl.ds(pl.multiple_of(jc * tk, tk), tk), :]
        s_ref[slot] = jnp.dot(kb, qop_ref[e], preferred_element_type=F32)

    def consume(w, s):
        e, j, jc, real = item(w)
        limit = jnp.where(real, i * tq - j * tk, -tq)
        p = jnp.exp2(jnp.where(key_minus_query <= limit, s, MASK_VALUE)).astype(BF16)
        vb = va_ref[e, :, pl.ds(pl.multiple_of(jc * tk, tk), tk)]
        acc_ref[e] += jnp.dot(vb, p, preferred_element_type=F32)

    ka_ref, va_ref = ka_ref.at[0], va_ref.at[0]
    for u in range(grp):
        build_scores(u, u)

    def stage(g, carry):
        for u in range(grp):
            s = s_ref[u]
            build_scores((g + 1) * grp + u, u)
            consume(g * grp + u, s)
        return carry

    lax.fori_loop(0, (n_items + grp - 1) // grp, stage, 0)
    outs = []
    for e in range(hpb):
        acc = acc_ref[e]
        outs.append(acc[:hd, :] / acc[hd:hd + 1, :])
    o = jnp.concatenate(outs, axis=0).T
    o_ref[0] = (o * sg_ref[0]).astype(o_ref.dtype)


def _fox_rowmax_body(lo_ref, qa_ref, ka_ref, m_ref, *, tq, tk, hd, nh):
    for e in range(LANES // hd):
        lo, diag, spb, scores = _fox_blocks(lo_ref, qa_ref, ka_ref, e, tq=tq, tk=tk, hd=hd, nh=nh)
        m = lax.fori_loop(lo, diag,
                          lambda j, m: jnp.maximum(m, jnp.max(scores(j, False), axis=0, keepdims=True)),
                          jnp.full((1, tq), MASK_VALUE, F32))
        for sb in range(spb):
            m = jnp.maximum(m, jnp.max(scores(diag + sb, True), axis=0, keepdims=True))
        m_ref[0, e] = jnp.broadcast_to(m, m_ref.shape[2:])


def _fox_attn(qa, ka, va, sg, c, bound):
    bsz, nh, _, t = qa.shape
    d = sg.shape[-1]
    hd = d // nh
    hpb = LANES // hd
    tq, tk = min(FOX_TQ, t), min(FOX_TK, t)
    c_start = c[:, 0::tq, :]
    c_end = c[:, tk - 1::tk, :]
    weightless = c_end[:, None, :, :] > (c_start + (2.0 * bound + FOX_SKIP_LOG))[:, :, None, :]
    first_block = jnp.sum(weightless, axis=2, dtype=jnp.int32).reshape(-1)
    smem = pl.BlockSpec(memory_space=pltpu.SMEM)
    grid = (bsz, nh // hpb, t // tq)
    qblk = pl.BlockSpec((1, hpb, LANES, tq), lambda b, p, i: (b, p, 0, i))
    kblk = pl.BlockSpec((1, hpb, t, LANES), lambda b, p, i: (b, p, 0, 0))
    static = dict(tq=tq, tk=tk, hd=hd, nh=nh)
    bound2 = bound * LOG2E

    def true_rowmax():
        m = pl.pallas_call(
            functools.partial(_fox_rowmax_body, **static),
            grid=grid,
            in_specs=[smem, qblk, kblk],
            out_specs=pl.BlockSpec((1, hpb, 8, tq), lambda b, p, i: (b, p, 0, i)),
            out_shape=jax.ShapeDtypeStruct((bsz, nh, 8, t), F32),
            compiler_params=_params("parallel", "parallel", "arbitrary"),
            name="fox_rowmax",
        )(first_block, qa, ka)
        return m[:, :, 0, :]

    ref = lax.cond(2.0 * bound2 <= FOX_FIXED_REF_MAX,
                   lambda: jnp.full((bsz, nh, t), bound2, F32), true_rowmax)
    pieces = jnp.stack([-p for p in _split3(ref)], axis=2)
    mr = jnp.pad(pieces, ((0, 0), (0, 0), (0, 16 - 3), (0, 0))).astype(BF16)
    oblk = pl.BlockSpec((1, tq, LANES), lambda b, p, i: (b, i, p))
    return pl.pallas_call(
        functools.partial(_fox_attn_body, **static),
        grid=grid,
        in_specs=[smem, qblk,
                  pl.BlockSpec((1, hpb, 16, tq), lambda b, p, i: (b, p, 0, i)),
                  kblk,
                  pl.BlockSpec((1, hpb, FOX_V_ROWS, t), lambda b, p, i: (b, p, 0, 0)),
                  oblk],
        out_specs=oblk,
        out_shape=jax.ShapeDtypeStruct((bsz, t, d), BF16),
        scratch_shapes=[pltpu.VMEM((hpb, LANES, tq), BF16), pltpu.VMEM((FOX_GROUP, tk, tq), F32),
                        pltpu.VMEM((hpb, FOX_V_ROWS, tq), F32)],
        compiler_params=_params("parallel", "parallel", "arbitrary"),
        name="fox_attn",
    )(first_block, qa, mr, ka, va, sg)


def _fox_mixer(x, bsz, t, g, w_in, b_f, gq, gk):
    n, d = x.shape
    hd = d // FOX_HEADS
    gq_s = gq * (LOG2E / math.sqrt(hd))
    qa, ka, va, sg, c = _fox_proj(x.reshape(bsz, t, d), g, w_in, b_f, gq_s, gk)
    bound = 1.02 * hd * jnp.max(jnp.abs(gq)) / math.sqrt(hd) * jnp.max(jnp.abs(gk))
    return _fox_attn(qa, ka, va, sg, c, bound).reshape(n, d)


def kernel(x, norm_g, ffn_w_in, ffn_w_out, hg_w_in, hg_lb_logits, hg_out_norm_g, hg_w_out,
           fox_w_in, fox_b_f, fox_q_norm_g, fox_k_norm_g, fox_w_out):
    bsz, t, d = x.shape
    depth = norm_g.shape[0]
    h = x.reshape(bsz * t, d)
    for layer in range(depth):
        j = layer // N_MIXERS
        h = _ffn(h, norm_g[layer, 0], ffn_w_in[layer, 0], ffn_w_out[layer, 0])
        if layer % N_MIXERS == 0:
            o = _hgrn_mixer(h, bsz, t, norm_g[layer, 1], hg_w_in[j], hg_lb_logits, j, hg_out_norm_g[j])
            w_o = hg_w_out[j]
        else:
            o = _fox_mixer(h, bsz, t, norm_g[layer, 1], fox_w_in[j], fox_b_f[j],
                           fox_q_norm_g[j], fox_k_norm_g[j])
            w_o = fox_w_out[j]
        h = _ffn(h, norm_g[layer, 2], ffn_w_in[layer, 1], ffn_w_out[layer, 1], mixer=(o, w_o))
    return h.reshape(bsz, t, d)
```

```python
import functools
import math

import jax
import jax.numpy as jnp
from jax import lax
from jax.experimental import pallas as pl
from jax.experimental.pallas import tpu as pltpu

F32 = jnp.float32
BF16 = jnp.bfloat16

EPS = 1e-6
MIN_GATE = 1e-30
MASK_VALUE = -1e30
HG_HEADS = 8
FOX_HEADS = 16
N_MIXERS = 2

LANES = 128
VMEM_LIMIT_BYTES = 56 * 1024 * 1024

_NT = (((1,), (1,)), ((), ()))
_TN = (((0,), (0,)), ((), ()))


def _params(*semantics):
    return pltpu.CompilerParams(dimension_semantics=semantics, vmem_limit_bytes=VMEM_LIMIT_BYTES)


def _rmsnorm(x, g):
    ms = jnp.mean(x * x, axis=-1, keepdims=True)
    return x * lax.rsqrt(ms + EPS) * g


_sigmoid = jax.nn.sigmoid


def _silu(y):
    return y * _sigmoid(y)


def _resident(shape):
    return pl.BlockSpec(shape, lambda *_: (0,) * len(shape), pipeline_mode=pl.Buffered(1))


def _ffn_body(*refs, d_ff, tf, has_mixer):
    if has_mixer:
        x_ref, mo_ref, mw_ref, g_ref, win_ref, wout_ref, o_ref, hn_ref, a_ref, xr_ref = refs
        xr_ref[...] = x_ref[...] + jnp.dot(mo_ref[...], mw_ref[...], preferred_element_type=F32)
        x_ref = xr_ref
    else:
        x_ref, g_ref, win_ref, wout_ref, o_ref, hn_ref, a_ref = refs
    hn_ref[...] = _rmsnorm(x_ref[...], g_ref[...]).astype(BF16)
    for f in range(d_ff // tf):
        hn = hn_ref[...]
        gate = jnp.dot(hn, win_ref[:, f * tf:(f + 1) * tf], preferred_element_type=F32)
        up = jnp.dot(hn, win_ref[:, d_ff + f * tf:d_ff + (f + 1) * tf], preferred_element_type=F32)
        a_ref[:, f * tf:(f + 1) * tf] = (_silu(gate) * up).astype(BF16)
    y = jnp.dot(a_ref[...], wout_ref[...], preferred_element_type=F32)
    o_ref[...] = x_ref[...] + 0.5 * y


def _ffn(x, g, w_in, w_out, mixer=None):
    n, d = x.shape
    d_ff = w_out.shape[0]
    tm = min(512, n)
    tf = 256
    row = pl.BlockSpec((tm, d), lambda i: (i, 0))
    args, specs, scratch = [x], [row], [pltpu.VMEM((tm, d), BF16), pltpu.VMEM((tm, d_ff), BF16)]
    if mixer is not None:
        args += [mixer[0], mixer[1].astype(BF16)]
        specs += [row, _resident((d, d))]
        scratch.append(pltpu.VMEM((tm, d), F32))
    return pl.pallas_call(
        functools.partial(_ffn_body, d_ff=d_ff, tf=tf, has_mixer=mixer is not None),
        grid=(n // tm,),
        in_specs=specs + [_resident((1, d)), _resident((d, 2 * d_ff)), _resident((d_ff, d))],
        out_specs=row,
        out_shape=jax.ShapeDtypeStruct((n, d), F32),
        scratch_shapes=scratch,
        compiler_params=_params("parallel"),
        name="ffn",
    )(*args, g.reshape(1, d), w_in.astype(BF16), w_out.astype(BF16))


HG_CHUNK = 128
HG_BAND = 4
HG_GROUP = 8
SUBLANES = 8


def _hg_body(x_ref, g_ref, w_ref, lbl_ref, gn_ref, o_ref, hn_ref, ya_ref, yb_ref, st_ref, *, layer_j, tb):
    c_len = HG_CHUNK
    dk = o_ref.shape[-1]
    t_blk, h = pl.program_id(1), pl.program_id(2)
    n_heads = pl.num_programs(2)

    @pl.when(h == 0)
    def _():
        hn_ref[...] = _rmsnorm(x_ref[0], g_ref[...]).astype(BF16)
        ya_ref[...] = jnp.dot(hn_ref[...], w_ref[0], preferred_element_type=F32)

    @pl.when(t_blk == 0)
    def _():
        st_ref[h] = jnp.zeros(st_ref.shape[1:], F32)

    n_layers = lbl_ref.shape[0]
    rows = [lbl_ref[i, 0] for i in range(n_layers)]
    mx = functools.reduce(jnp.maximum, rows)
    es = [jnp.exp(r - mx) for r in rows]
    den = functools.reduce(jnp.add, es)
    ps = [e / den for e in es]
    lb = functools.reduce(jnp.add, ps[:layer_j + 1]) - ps[0]
    next_head = jnp.minimum(h + 1, n_heads - 1)

    row = lax.broadcasted_iota(jnp.int32, (c_len, 1), 0)
    rr = lax.broadcasted_iota(jnp.int32, (c_len, c_len), 0)
    cc = lax.broadcasted_iota(jnp.int32, (c_len, c_len), 1)
    tri = (cc <= rr).astype(BF16)
    levels = []
    m = c_len // 2
    while m >= HG_BAND:
        levels.append(m)
        m //= 2
    diff = jnp.where(cc < rr, rr ^ cc, 0)
    pair_level = functools.reduce(jnp.add, [(diff >= m).astype(jnp.int32) for m in levels])
    band_ok = [(row % HG_BAND) >= dd for dd in range(HG_BAND)]

    def roll_in_tile(a, dd):
        a3 = a.reshape(c_len // SUBLANES, SUBLANES, dk)
        return pltpu.roll(a3, dd, 1).reshape(c_len, dk)

    def chunk(y_ref, r0):
        r0 = pl.multiple_of(r0, c_len)
        y = y_ref[pl.ds(r0, c_len), :]
        q = _silu(y[:, :dk])
        z = y[:, dk:2 * dk]
        v = y[:, 2 * dk:3 * dk]
        gs = _silu(y[:, 3 * dk:])
        lf = jnp.log(jnp.maximum(lb + (1.0 - lb) * _sigmoid(z), MIN_GATE))
        k = (1.0 - lb) * _sigmoid(-z)
        lf_hi = lf.astype(BF16)
        lf_lo = (lf - lf_hi.astype(F32)).astype(BF16)
        b = (jnp.dot(tri, lf_hi, preferred_element_type=F32)
             + jnp.dot(tri, lf_lo, preferred_element_type=F32))
        b_last = b[c_len - 1:c_len, :]
        st = st_ref[h]
        o = lax.dot_general((q * jnp.exp(b)).astype(BF16), st.astype(BF16), _NT,
                            preferred_element_type=F32)
        sc = jnp.zeros((c_len, c_len), F32)
        for idx, m in enumerate(levels):
            b3 = b.reshape(c_len // (2 * m), 2 * m, dk)
            ref = jnp.broadcast_to(b3[:, m:m + 1, :], b3.shape).reshape(c_len, dk)
            e = jnp.exp(-jnp.abs(b - ref))
            s_m = lax.dot_general((q * e).astype(BF16), (k * e).astype(BF16), _NT,
                                  preferred_element_type=F32)
            sc = jnp.where(pair_level == len(levels) - idx, s_m, sc)
        o = o + jnp.dot(sc.astype(BF16), v.astype(BF16), preferred_element_type=F32)
        o = o + jnp.sum(q * k, axis=-1, keepdims=True) * v
        for dd in range(1, HG_BAND):
            kd = roll_in_tile(k, dd)
            bd = roll_in_tile(b, dd)
            vd = roll_in_tile(v, dd)
            w = jnp.exp(jnp.minimum(b - bd, 0.0))
            s_d = jnp.sum(q * kd * w, axis=-1, keepdims=True)
            o = o + jnp.where(band_ok[dd], s_d, 0.0) * vd
        k_dec = (k * jnp.exp(b_last - b)).astype(BF16)
        st_ref[h] = st * jnp.exp(b_last) + lax.dot_general(v.astype(BF16), k_dec, _TN,
                                                           preferred_element_type=F32)
        ms = jnp.mean(o * o, axis=-1, keepdims=True)
        o_ref[0, pl.ds(r0, c_len), :] = (o * lax.rsqrt(ms + EPS) * gn_ref[...] * gs).astype(o_ref.dtype)

    group_rows = HG_GROUP * c_len

    def run(cur_ref, nxt_ref):
        def group(gi, carry):
            g0 = pl.multiple_of(gi * group_rows, group_rows)
            nxt_ref[pl.ds(g0, group_rows), :] = jnp.dot(hn_ref[pl.ds(g0, group_rows), :], w_ref[next_head],
                                                        preferred_element_type=F32)
            for ci in range(HG_GROUP):
                chunk(cur_ref, g0 + ci * c_len)
            return carry

        lax.fori_loop(0, tb // group_rows, group, 0)

    @pl.when(h % 2 == 0)
    def _():
        run(ya_ref, yb_ref)

    @pl.when(h % 2 == 1)
    def _():
        run(yb_ref, ya_ref)


def _hgrn_mixer(x, bsz, t, g, w_in, lb_logits, layer_j, out_norm_g):
    n, d = x.shape
    nh = HG_HEADS
    dk = d // nh
    tb = min(2048, t)
    assert tb % (HG_GROUP * HG_CHUNK) == 0 and nh % 2 == 0
    w_heads = w_in.reshape(d, 4, nh, dk).transpose(2, 0, 1, 3).reshape(nh, d, 4 * dk).astype(BF16)
    n_layers = lb_logits.shape[0]
    o = pl.pallas_call(
        functools.partial(_hg_body, layer_j=layer_j, tb=tb),
        grid=(bsz, t // tb, nh),
        in_specs=[pl.BlockSpec((1, tb, d), lambda b, i, h: (b, i, 0)),
                  _resident((1, d)),
                  _resident((nh, d, 4 * dk)),
                  pl.BlockSpec((n_layers, 1, 1, dk), lambda b, i, h: (0, h, 0, 0)),
                  _resident((1, dk))],
        out_specs=pl.BlockSpec((1, tb, dk), lambda b, i, h: (b, i, h)),
        out_shape=jax.ShapeDtypeStruct((bsz, t, d), BF16),
        scratch_shapes=[pltpu.VMEM((tb, d), BF16), pltpu.VMEM((tb, 4 * dk), F32),
                        pltpu.VMEM((tb, 4 * dk), F32), pltpu.VMEM((nh, dk, dk), F32)],
        compiler_params=_params("parallel", "arbitrary", "arbitrary"),
        name="hgrn",
    )(x.reshape(bsz, t, d), g.reshape(1, d), w_heads, lb_logits.reshape(n_layers, nh, 1, dk),
      out_norm_g.reshape(1, dk))
    return o.reshape(n, d)


def _head_rmsnorm(y, g2, ones2, hd):
    sq = y * y
    hi = sq.astype(BF16)
    lo = (sq - hi.astype(F32)).astype(BF16)
    ss = jnp.dot(hi, ones2, preferred_element_type=F32) + jnp.dot(lo, ones2, preferred_element_type=F32)
    return y * lax.rsqrt(ss * (1.0 / hd) + EPS) * g2


def _split3(x):
    hi = x.astype(BF16).astype(F32)
    r1 = x - hi
    mid = r1.astype(BF16).astype(F32)
    lo = (r1 - mid).astype(BF16).astype(F32)
    return hi, mid, lo


FOX_AUG_C = 0
FOX_AUG_ONE = 3
FOX_AUG_REF = 16
FOX_PIECE_STRIDE = 16
FOX_ONE_LANE = 48
FOX_V_ROWS = 80


def _fox_placement(nh, hd):
    import numpy as np
    hpb = LANES // hd
    assert hpb == 2 and 3 * FOX_PIECE_STRIDE <= FOX_ONE_LANE and nh <= FOX_PIECE_STRIDE
    pk = np.zeros((nh // hpb, LANES, hpb * LANES), np.float32)
    pq = np.zeros((nh // hpb, hpb * LANES, LANES), np.float32)
    for p in range(nh // hpb):
        for e in range(hpb):
            h = p * hpb + e
            base = e * LANES + (hd if e == 0 else 0)
            for i in range(3):
                pk[p, FOX_PIECE_STRIDE * i + h, base + FOX_AUG_ONE + i] = -1.0
                pk[p, FOX_ONE_LANE, base + FOX_AUG_C + i] = 1.0
                pk[p, FOX_ONE_LANE, base + FOX_AUG_REF + i] = 1.0
                pq[p, base + FOX_AUG_C + i, FOX_PIECE_STRIDE * i + h] = 1.0
                pq[p, base + FOX_AUG_ONE + i, FOX_ONE_LANE] = 1.0
    return jnp.asarray(pk, BF16), jnp.asarray(pq, BF16)


def _fox_proj_body(x_ref, g_ref, w_ref, wf_ref, bf_ref, gq_ref, gk_ref, pk_ref, pq_ref,
                   qa_ref, ka_ref, va_ref, sg_ref, c_ref, hn_ref, carry_ref, *, hd, nh):
    tm, d = x_ref.shape[1], x_ref.shape[2]
    hn_ref[...] = _rmsnorm(x_ref[0], g_ref[...]).astype(BF16)

    @pl.when(pl.program_id(1) == 0)
    def _():
        carry_ref[...] = jnp.zeros_like(carry_ref)

    z = jnp.dot(hn_ref[...], wf_ref[...], preferred_element_type=F32) + bf_ref[...]
    lf = jnp.minimum(z, 0.0) - jnp.log1p(jnp.exp(-jnp.abs(z)))
    tri = (lax.broadcasted_iota(jnp.int32, (tm, tm), 1)
           <= lax.broadcasted_iota(jnp.int32, (tm, tm), 0)).astype(BF16)
    c = carry_ref[...]
    for piece in _split3(lf):
        c = c + jnp.dot(tri, piece.astype(BF16), preferred_element_type=F32)
    carry_ref[...] = c[tm - 1:tm, :]
    c_ref[0] = c[:, :nh]
    hi, mid, lo = _split3(c * LOG2E)
    lane = lax.broadcasted_iota(jnp.int32, (tm, LANES), 1)
    pieces = jnp.where(lane < FOX_PIECE_STRIDE, hi,
                       jnp.where(lane < 2 * FOX_PIECE_STRIDE, mid,
                                 jnp.where(lane < 3 * FOX_PIECE_STRIDE, lo,
                                           jnp.where(lane == FOX_ONE_LANE, 1.0, 0.0))))
    pieces_t = pieces.T.astype(BF16)
    pieces = pieces.astype(BF16)

    ri = lax.broadcasted_iota(jnp.int32, (LANES, LANES), 0)
    ci = lax.broadcasted_iota(jnp.int32, (LANES, LANES), 1)
    ones2 = ((ri // hd) == (ci // hd)).astype(BF16)
    ones_rows = jnp.ones((FOX_V_ROWS - hd, tm), F32)
    hpb = LANES // hd
    wide = 2 * LANES
    for part in range(4):
        for cw in range(d // wide):
            yw = jnp.dot(hn_ref[...], w_ref[:, part * d + cw * wide:part * d + (cw + 1) * wide],
                         preferred_element_type=F32)
            if part == 3:
                sg_ref[0, :, cw * wide:(cw + 1) * wide] = _sigmoid(yw)
                continue
            for half in range(wide // LANES):
                p = cw * (wide // LANES) + half
                y = yw[:, half * LANES:(half + 1) * LANES]
                if part == 1:
                    kn = _head_rmsnorm(y, gk_ref[...], ones2, hd)
                    aug = jnp.dot(pieces, pk_ref[p], preferred_element_type=F32)
                    for e in range(hpb):
                        ka_ref[0, p * hpb + e] = jnp.where((lane // hd) == e, kn,
                                                           aug[:, e * LANES:(e + 1) * LANES]).astype(BF16)
                    continue
                y_t = y.T
                if part == 2:
                    for e in range(hpb):
                        va_ref[0, p * hpb + e] = jnp.concatenate(
                            [y_t[e * hd:(e + 1) * hd], ones_rows], axis=0).astype(BF16)
                    continue
                aug_t = jnp.dot(pq_ref[p], pieces_t, preferred_element_type=F32)
                for e in range(hpb):
                    rows = y_t[e * hd:(e + 1) * hd]
                    ms = jnp.mean(rows * rows, axis=0, keepdims=True)
                    qn = rows * lax.rsqrt(ms + EPS) * gq_ref[...]
                    base = e * LANES + (hd if e == 0 else 0)
                    bias = aug_t[base:base + LANES - hd]
                    qa_ref[0, p * hpb + e] = jnp.concatenate([qn, bias] if e == 0 else [bias, qn],
                                                             axis=0).astype(BF16)


def _fox_proj(x, g, w_in, b_f, gq, gk):
    bsz, t, d = x.shape
    nh = FOX_HEADS
    hd = d // nh
    tm = min(512, t)
    pk, pq = _fox_placement(nh, hd)
    reps = FOX_ONE_LANE // FOX_PIECE_STRIDE
    w_f = jnp.pad(jnp.tile(jnp.pad(w_in[:, 4 * d:], ((0, 0), (0, FOX_PIECE_STRIDE - nh))), (1, reps)),
                  ((0, 0), (0, LANES - FOX_ONE_LANE)))
    b_pad = jnp.pad(jnp.tile(jnp.pad(b_f, (0, FOX_PIECE_STRIDE - nh)), reps), (0, LANES - FOX_ONE_LANE))
    row = pl.BlockSpec((1, tm, d), lambda b, i: (b, i, 0))
    return pl.pallas_call(
        functools.partial(_fox_proj_body, hd=hd, nh=nh),
        grid=(bsz, t // tm),
        in_specs=[row, _resident((1, d)), _resident((d, 4 * d)), _resident((d, LANES)),
                  _resident((1, LANES)), _resident((hd, 1)), _resident((1, LANES)),
                  _resident(pk.shape), _resident(pq.shape)],
        out_specs=[pl.BlockSpec((1, nh, LANES, tm), lambda b, i: (b, 0, 0, i)),
                   pl.BlockSpec((1, nh, tm, LANES), lambda b, i: (b, 0, i, 0)),
                   pl.BlockSpec((1, nh, FOX_V_ROWS, tm), lambda b, i: (b, 0, 0, i)),
                   row,
                   pl.BlockSpec((1, tm, nh), lambda b, i: (b, i, 0))],
        out_shape=[jax.ShapeDtypeStruct((bsz, nh, LANES, t), BF16),
                   jax.ShapeDtypeStruct((bsz, nh, t, LANES), BF16),
                   jax.ShapeDtypeStruct((bsz, nh, FOX_V_ROWS, t), BF16),
                   jax.ShapeDtypeStruct((bsz, t, d), F32),
                   jax.ShapeDtypeStruct((bsz, t, nh), F32)],
        scratch_shapes=[pltpu.VMEM((tm, d), BF16), pltpu.VMEM((1, LANES), F32)],
        compiler_params=_params("parallel", "arbitrary"),
        name="fox_proj",
    )(x, g.reshape(1, d), w_in[:, :4 * d].astype(BF16), w_f.astype(BF16), b_pad.reshape(1, LANES),
      gq.reshape(hd, 1), jnp.tile(gk, LANES // hd).reshape(1, LANES), pk, pq)


LOG2E = 1.4426950408889634
FOX_TQ = 512
FOX_TK = 256
FOX_SKIP_LOG = 105.0
FOX_FIXED_REF_MAX = 60.0


def _fox_blocks(lo_ref, qa_ref, ka_ref, e, *, tq, tk, hd, nh, ref_rows=None):
    b, pr, i = pl.program_id(0), pl.program_id(1), pl.program_id(2)
    spb = tq // tk
    head = pr * (LANES // hd) + e
    qa = qa_ref[0, e]
    if ref_rows is not None:
        r0 = (hd if e == 0 else 0) + FOX_AUG_REF
        qa = jnp.concatenate([qa[:r0], ref_rows, qa[r0 + ref_rows.shape[0]:]], axis=0)
    lo = lo_ref[(b * pl.num_programs(2) + i) * nh + head]
    rr = lax.broadcasted_iota(jnp.int32, (tk, tq), 0)
    cc = lax.broadcasted_iota(jnp.int32, (tk, tq), 1)

    def scores(j, masked):
        r0 = pl.multiple_of(j * tk, tk)
        s = jnp.dot(ka_ref[0, e, pl.ds(r0, tk), :], qa, preferred_element_type=F32)
        if masked:
            s = jnp.where(r0 + rr <= i * tq + cc, s, MASK_VALUE)
        return s

    return lo, i * spb, spb, scores


FOX_GROUP = 4


def _fox_attn_body(lo_ref, qa_ref, mr_ref, ka_ref, va_ref, sg_ref, o_ref,
                   qop_ref, s_ref, acc_ref, *, tq, tk, hd, nh):
    i = pl.program_id(2)
    hpb = LANES // hd
    spb = tq // tk
    n_key_blocks = pl.num_programs(2) * spb
    grp = s_ref.shape[0]
    los = []
    for e in range(hpb):
        lo, diag, _, _ = _fox_blocks(lo_ref, qa_ref, ka_ref, e, tq=tq, tk=tk, hd=hd, nh=nh)
        los.append(lo)
        r0 = (hd if e == 0 else 0) + FOX_AUG_REF
        qa = qa_ref[0, e]
        qop_ref[e] = jnp.concatenate([qa[:r0], mr_ref[0, e], qa[r0 + mr_ref.shape[2]:]], axis=0)
    counts = [diag + spb - lo for lo in los]
    n_items = functools.reduce(jnp.add, counts)
    acc_ref[...] = jnp.zeros_like(acc_ref)
    key_minus_query = (lax.broadcasted_iota(jnp.int32, (tk, tq), 0)
                       - lax.broadcasted_iota(jnp.int32, (tk, tq), 1))

    def item(w):
        e, j = jnp.int32(0), los[0] + w
        first = jnp.int32(0)
        for h in range(1, hpb):
            first = first + counts[h - 1]
            e = jnp.where(w >= first, h, e)
            j = jnp.where(w >= first, los[h] + (w - first), j)
        return e, j, jnp.minimum(j, n_key_blocks - 1), w < n_items

    def scores(w):
        e, _, jc, _ = item(w)
        kb = ka_ref[e, pl.ds(pl.multiple_of(jc * tk, tk), tk), :]
        return jnp.dot(kb, qop_ref[e], preferred_element_type=F32)

    def build_scores(w, slot):
        s_ref[slot] = scores(w)

    def consume(w, s):
        e, j, jc, real = item(w)
        limit = jnp.where(real, i * tq - j * tk, -tq)
        p = jnp.exp2(jnp.where(key_minus_query <= limit, s, MASK_VALUE)).astype(BF16)
        vb = va_ref[e, :, pl.ds(pl.multiple_of(jc * tk, tk), tk)]
        acc_ref[e] += jnp.dot(vb, p, preferred_element_type=F32)

    ka_ref, va_ref = ka_ref.at[0], va_ref.at[0]
    first = [scores(u) for u in range(grp)]
    for u in range(grp):
        build_scores(grp + u, u)
        consume(u, first[u])

    def stage(g, carry):
        for u in range(grp):
            s = s_ref[u]
            build_scores((g + 1) * grp + u, u)
            consume(g * grp + u, s)
        return carry

    lax.fori_loop(1, (n_items + grp - 1) // grp, stage, 0)
    outs = []
    for e in range(hpb):
        acc = acc_ref[e]
        outs.append(acc[:hd, :] / acc[hd:hd + 1, :])
    o = jnp.concatenate(outs, axis=0).T
    o_ref[0] = (o * sg_ref[0]).astype(o_ref.dtype)


def _fox_rowmax_body(lo_ref, qa_ref, ka_ref, m_ref, *, tq, tk, hd, nh):
    for e in range(LANES // hd):
        lo, diag, spb, scores = _fox_blocks(lo_ref, qa_ref, ka_ref, e, tq=tq, tk=tk, hd=hd, nh=nh)
        m = lax.fori_loop(lo, diag,
                          lambda j, m: jnp.maximum(m, jnp.max(scores(j, False), axis=0, keepdims=True)),
                          jnp.full((1, tq), MASK_VALUE, F32))
        for sb in range(spb):
            m = jnp.maximum(m, jnp.max(scores(diag + sb, True), axis=0, keepdims=True))
        m_ref[0, e] = jnp.broadcast_to(m, m_ref.shape[2:])


def _fox_attn(qa, ka, va, sg, c, bound):
    bsz, nh, _, t = qa.shape
    d = sg.shape[-1]
    hd = d // nh
    hpb = LANES // hd
    tq, tk = min(FOX_TQ, t), min(FOX_TK, t)
    c_start = c[:, 0::tq, :]
    c_end = c[:, tk - 1::tk, :]
    weightless = c_end[:, None, :, :] > (c_start + (2.0 * bound + FOX_SKIP_LOG))[:, :, None, :]
    first_block = jnp.sum(weightless, axis=2, dtype=jnp.int32).reshape(-1)
    smem = pl.BlockSpec(memory_space=pltpu.SMEM)
    grid = (bsz, nh // hpb, t // tq)
    qblk = pl.BlockSpec((1, hpb, LANES, tq), lambda b, p, i: (b, p, 0, i))
    kblk = pl.BlockSpec((1, hpb, t, LANES), lambda b, p, i: (b, p, 0, 0))
    static = dict(tq=tq, tk=tk, hd=hd, nh=nh)
    bound2 = bound * LOG2E

    def true_rowmax():
        m = pl.pallas_call(
            functools.partial(_fox_rowmax_body, **static),
            grid=grid,
            in_specs=[smem, qblk, kblk],
            out_specs=pl.BlockSpec((1, hpb, 8, tq), lambda b, p, i: (b, p, 0, i)),
            out_shape=jax.ShapeDtypeStruct((bsz, nh, 8, t), F32),
            compiler_params=_params("parallel", "parallel", "arbitrary"),
            name="fox_rowmax",
        )(first_block, qa, ka)
        return m[:, :, 0, :]

    ref = lax.cond(2.0 * bound2 <= FOX_FIXED_REF_MAX,
                   lambda: jnp.full((bsz, nh, t), bound2, F32), true_rowmax)
    pieces = jnp.stack([-p for p in _split3(ref)], axis=2)
    mr = jnp.pad(pieces, ((0, 0), (0, 0), (0, 16 - 3), (0, 0))).astype(BF16)
    oblk = pl.BlockSpec((1, tq, LANES), lambda b, p, i: (b, i, p))
    return pl.pallas_call(
        functools.partial(_fox_attn_body, **static),
        grid=grid,
        in_specs=[smem, qblk,
                  pl.BlockSpec((1, hpb, 16, tq), lambda b, p, i: (b, p, 0, i)),
                  kblk,
                  pl.BlockSpec((1, hpb, FOX_V_ROWS, t), lambda b, p, i: (b, p, 0, 0)),
                  oblk],
        out_specs=oblk,
        out_shape=jax.ShapeDtypeStruct((bsz, t, d), BF16),
        scratch_shapes=[pltpu.VMEM((hpb, LANES, tq), BF16), pltpu.VMEM((FOX_GROUP, tk, tq), F32),
                        pltpu.VMEM((hpb, FOX_V_ROWS, tq), F32)],
        compiler_params=_params("parallel", "parallel", "arbitrary"),
        name="fox_attn",
    )(first_block, qa, mr, ka, va, sg)


def _fox_mixer(x, bsz, t, g, w_in, b_f, gq, gk):
    n, d = x.shape
    hd = d // FOX_HEADS
    gq_s = gq * (LOG2E / math.sqrt(hd))
    qa, ka, va, sg, c = _fox_proj(x.reshape(bsz, t, d), g, w_in, b_f, gq_s, gk)
    bound = 1.02 * hd * jnp.max(jnp.abs(gq)) / math.sqrt(hd) * jnp.max(jnp.abs(gk))
    return _fox_attn(qa, ka, va, sg, c, bound).reshape(n, d)


def kernel(x, norm_g, ffn_w_in, ffn_w_out, hg_w_in, hg_lb_logits, hg_out_norm_g, hg_w_out,
           fox_w_in, fox_b_f, fox_q_norm_g, fox_k_norm_g, fox_w_out):
    bsz, t, d = x.shape
    depth = norm_g.shape[0]
    h = x.reshape(bsz * t, d)
    for layer in range(depth):
        j = layer // N_MIXERS
        h = _ffn(h, norm_g[layer, 0], ffn_w_in[layer, 0], ffn_w_out[layer, 0])
        if layer % N_MIXERS == 0:
            o = _hgrn_mixer(h, bsz, t, norm_g[layer, 1], hg_w_in[j], hg_lb_logits, j, hg_out_norm_g[j])
            w_o = hg_w_out[j]
        else:
            o = _fox_mixer(h, bsz, t, norm_g[layer, 1], fox_w_in[j], fox_b_f[j],
                           fox_q_norm_g[j], fox_k_norm_g[j])
            w_o = fox_w_out[j]
        h = _ffn(h, norm_g[layer, 2], ffn_w_in[layer, 1], ffn_w_out[layer, 1], mixer=(o, w_o))
    return h.reshape(bsz, t, d)
```

```python
import functools
import math

import jax
import jax.numpy as jnp
from jax import lax
from jax.experimental import pallas as pl
from jax.experimental.pallas import tpu as pltpu

F32 = jnp.float32
BF16 = jnp.bfloat16

EPS = 1e-6
MIN_GATE = 1e-30
MASK_VALUE = -1e30
HG_HEADS = 8
FOX_HEADS = 16
N_MIXERS = 2

LANES = 128
VMEM_LIMIT_BYTES = 56 * 1024 * 1024

_NT = (((1,), (1,)), ((), ()))
_TN = (((0,), (0,)), ((), ()))


def _params(*semantics):
    return pltpu.CompilerParams(dimension_semantics=semantics, vmem_limit_bytes=VMEM_LIMIT_BYTES)


def _rmsnorm(x, g):
    ms = jnp.mean(x * x, axis=-1, keepdims=True)
    return x * lax.rsqrt(ms + EPS) * g


_sigmoid = jax.nn.sigmoid


def _silu(y):
    return y * _sigmoid(y)


def _resident(shape):
    return pl.BlockSpec(shape, lambda *_: (0,) * len(shape), pipeline_mode=pl.Buffered(1))


def _ffn_body(*refs, d_ff, tf, has_mixer):
    if has_mixer:
        x_ref, mo_ref, mw_ref, g_ref, win_ref, wout_ref, o_ref, hn_ref, a_ref, xr_ref = refs
        xr_ref[...] = x_ref[...] + jnp.dot(mo_ref[...], mw_ref[...], preferred_element_type=F32)
        x_ref = xr_ref
    else:
        x_ref, g_ref, win_ref, wout_ref, o_ref, hn_ref, a_ref = refs
    hn_ref[...] = _rmsnorm(x_ref[...], g_ref[...]).astype(BF16)
    for f in range(d_ff // tf):
        hn = hn_ref[...]
        gate = jnp.dot(hn, win_ref[:, f * tf:(f + 1) * tf], preferred_element_type=F32)
        up = jnp.dot(hn, win_ref[:, d_ff + f * tf:d_ff + (f + 1) * tf], preferred_element_type=F32)
        a_ref[:, f * tf:(f + 1) * tf] = (_silu(gate) * up).astype(BF16)
    y = jnp.dot(a_ref[...], wout_ref[...], preferred_element_type=F32)
    o_ref[...] = x_ref[...] + 0.5 * y


def _ffn(x, g, w_in, w_out, mixer=None):
    n, d = x.shape
    d_ff = w_out.shape[0]
    tm = min(512, n)
    tf = 256
    row = pl.BlockSpec((tm, d), lambda i: (i, 0))
    args, specs, scratch = [x], [row], [pltpu.VMEM((tm, d), BF16), pltpu.VMEM((tm, d_ff), BF16)]
    if mixer is not None:
        args += [mixer[0], mixer[1].astype(BF16)]
        specs += [row, _resident((d, d))]
        scratch.append(pltpu.VMEM((tm, d), F32))
    return pl.pallas_call(
        functools.partial(_ffn_body, d_ff=d_ff, tf=tf, has_mixer=mixer is not None),
        grid=(n // tm,),
        in_specs=specs + [_resident((1, d)), _resident((d, 2 * d_ff)), _resident((d_ff, d))],
        out_specs=row,
        out_shape=jax.ShapeDtypeStruct((n, d), F32),
        scratch_shapes=scratch,
        compiler_params=_params("parallel"),
        name="ffn",
    )(*args, g.reshape(1, d), w_in.astype(BF16), w_out.astype(BF16))


HG_CHUNK = 128
HG_BAND = 4
HG_GROUP = 16
SUBLANES = 8


def _hg_body(x_ref, g_ref, w_ref, lbl_ref, gn_ref, o_ref, hn_ref, ya_ref, yb_ref, st_ref, *, layer_j, tb):
    c_len = HG_CHUNK
    dk = o_ref.shape[-1]
    t_blk, h = pl.program_id(1), pl.program_id(2)
    n_heads = pl.num_programs(2)

    @pl.when(h == 0)
    def _():
        hn_ref[...] = _rmsnorm(x_ref[0], g_ref[...]).astype(BF16)
        ya_ref[...] = jnp.dot(hn_ref[...], w_ref[0], preferred_element_type=F32)

    @pl.when(t_blk == 0)
    def _():
        st_ref[h] = jnp.zeros(st_ref.shape[1:], F32)

    n_layers = lbl_ref.shape[0]
    rows = [lbl_ref[i, 0] for i in range(n_layers)]
    mx = functools.reduce(jnp.maximum, rows)
    es = [jnp.exp(r - mx) for r in rows]
    den = functools.reduce(jnp.add, es)
    ps = [e / den for e in es]
    lb = functools.reduce(jnp.add, ps[:layer_j + 1]) - ps[0]
    next_head = jnp.minimum(h + 1, n_heads - 1)

    row = lax.broadcasted_iota(jnp.int32, (c_len, 1), 0)
    rr = lax.broadcasted_iota(jnp.int32, (c_len, c_len), 0)
    cc = lax.broadcasted_iota(jnp.int32, (c_len, c_len), 1)
    tri = (cc <= rr).astype(BF16)
    levels = []
    m = c_len // 2
    while m >= HG_BAND:
        levels.append(m)
        m //= 2
    diff = jnp.where(cc < rr, rr ^ cc, 0)
    pair_level = functools.reduce(jnp.add, [(diff >= m).astype(jnp.int32) for m in levels])
    band_ok = [(row % HG_BAND) >= dd for dd in range(HG_BAND)]

    def roll_in_tile(a, dd):
        a3 = a.reshape(c_len // SUBLANES, SUBLANES, dk)
        return pltpu.roll(a3, dd, 1).reshape(c_len, dk)

    def chunk(y_ref, r0):
        r0 = pl.multiple_of(r0, c_len)
        y = y_ref[pl.ds(r0, c_len), :]
        q = _silu(y[:, :dk])
        z = y[:, dk:2 * dk]
        v = y[:, 2 * dk:3 * dk]
        gs = _silu(y[:, 3 * dk:])
        lf = jnp.log(jnp.maximum(lb + (1.0 - lb) * _sigmoid(z), MIN_GATE))
        k = (1.0 - lb) * _sigmoid(-z)
        lf_hi = lf.astype(BF16)
        lf_lo = (lf - lf_hi.astype(F32)).astype(BF16)
        b = (jnp.dot(tri, lf_hi, preferred_element_type=F32)
             + jnp.dot(tri, lf_lo, preferred_element_type=F32))
        b_last = b[c_len - 1:c_len, :]
        st = st_ref[h]
        o = lax.dot_general((q * jnp.exp(b)).astype(BF16), st.astype(BF16), _NT,
                            preferred_element_type=F32)
        sc = jnp.zeros((c_len, c_len), F32)
        for idx, m in enumerate(levels):
            b3 = b.reshape(c_len // (2 * m), 2 * m, dk)
            ref = jnp.broadcast_to(b3[:, m:m + 1, :], b3.shape).reshape(c_len, dk)
            e = jnp.exp(-jnp.abs(b - ref))
            s_m = lax.dot_general((q * e).astype(BF16), (k * e).astype(BF16), _NT,
                                  preferred_element_type=F32)
            sc = jnp.where(pair_level == len(levels) - idx, s_m, sc)
        o = o + jnp.dot(sc.astype(BF16), v.astype(BF16), preferred_element_type=F32)
        o = o + jnp.sum(q * k, axis=-1, keepdims=True) * v
        for dd in range(1, HG_BAND):
            kd = roll_in_tile(k, dd)
            bd = roll_in_tile(b, dd)
            vd = roll_in_tile(v, dd)
            w = jnp.exp(jnp.minimum(b - bd, 0.0))
            s_d = jnp.sum(q * kd * w, axis=-1, keepdims=True)
            o = o + jnp.where(band_ok[dd], s_d, 0.0) * vd
        k_dec = (k * jnp.exp(b_last - b)).astype(BF16)
        st_ref[h] = st * jnp.exp(b_last) + lax.dot_general(v.astype(BF16), k_dec, _TN,
                                                           preferred_element_type=F32)
        ms = jnp.mean(o * o, axis=-1, keepdims=True)
        o_ref[0, pl.ds(r0, c_len), :] = (o * lax.rsqrt(ms + EPS) * gn_ref[...] * gs).astype(o_ref.dtype)

    group_rows = HG_GROUP * c_len

    def run(cur_ref, nxt_ref):
        def group(gi, carry):
            g0 = pl.multiple_of(gi * group_rows, group_rows)
            nxt_ref[pl.ds(g0, group_rows), :] = jnp.dot(hn_ref[pl.ds(g0, group_rows), :], w_ref[next_head],
                                                        preferred_element_type=F32)
            for ci in range(HG_GROUP):
                chunk(cur_ref, g0 + ci * c_len)
            return carry

        lax.fori_loop(0, tb // group_rows, group, 0)

    @pl.when(h % 2 == 0)
    def _():
        run(ya_ref, yb_ref)

    @pl.when(h % 2 == 1)
    def _():
        run(yb_ref, ya_ref)


def _hgrn_mixer(x, bsz, t, g, w_in, lb_logits, layer_j, out_norm_g):
    n, d = x.shape
    nh = HG_HEADS
    dk = d // nh
    tb = min(2048, t)
    assert tb % (HG_GROUP * HG_CHUNK) == 0 and nh % 2 == 0
    w_heads = w_in.reshape(d, 4, nh, dk).transpose(2, 0, 1, 3).reshape(nh, d, 4 * dk).astype(BF16)
    n_layers = lb_logits.shape[0]
    o = pl.pallas_call(
        functools.partial(_hg_body, layer_j=layer_j, tb=tb),
        grid=(bsz, t // tb, nh),
        in_specs=[pl.BlockSpec((1, tb, d), lambda b, i, h: (b, i, 0)),
                  _resident((1, d)),
                  _resident((nh, d, 4 * dk)),
                  pl.BlockSpec((n_layers, 1, 1, dk), lambda b, i, h: (0, h, 0, 0)),
                  _resident((1, dk))],
        out_specs=pl.BlockSpec((1, tb, dk), lambda b, i, h: (b, i, h)),
        out_shape=jax.ShapeDtypeStruct((bsz, t, d), BF16),
        scratch_shapes=[pltpu.VMEM((tb, d), BF16), pltpu.VMEM((tb, 4 * dk), F32),
                        pltpu.VMEM((tb, 4 * dk), F32), pltpu.VMEM((nh, dk, dk), F32)],
        compiler_params=_params("parallel", "arbitrary", "arbitrary"),
        name="hgrn",
    )(x.reshape(bsz, t, d), g.reshape(1, d), w_heads, lb_logits.reshape(n_layers, nh, 1, dk),
      out_norm_g.reshape(1, dk))
    return o.reshape(n, d)


def _head_rmsnorm(y, g2, ones2, hd):
    sq = y * y
    hi = sq.astype(BF16)
    lo = (sq - hi.astype(F32)).astype(BF16)
    ss = jnp.dot(hi, ones2, preferred_element_type=F32) + jnp.dot(lo, ones2, preferred_element_type=F32)
    return y * lax.rsqrt(ss * (1.0 / hd) + EPS) * g2


def _split3(x):
    hi = x.astype(BF16).astype(F32)
    r1 = x - hi
    mid = r1.astype(BF16).astype(F32)
    lo = (r1 - mid).astype(BF16).astype(F32)
    return hi, mid, lo


FOX_AUG_C = 0
FOX_AUG_ONE = 3
FOX_AUG_REF = 16
FOX_PIECE_STRIDE = 16
FOX_ONE_LANE = 48
FOX_V_ROWS = 80


def _fox_placement(nh, hd):
    import numpy as np
    hpb = LANES // hd
    assert hpb == 2 and 3 * FOX_PIECE_STRIDE <= FOX_ONE_LANE and nh <= FOX_PIECE_STRIDE
    pk = np.zeros((nh // hpb, LANES, hpb * LANES), np.float32)
    pq = np.zeros((nh // hpb, hpb * LANES, LANES), np.float32)
    for p in range(nh // hpb):
        for e in range(hpb):
            h = p * hpb + e
            base = e * LANES + (hd if e == 0 else 0)
            for i in range(3):
                pk[p, FOX_PIECE_STRIDE * i + h, base + FOX_AUG_ONE + i] = -1.0
                pk[p, FOX_ONE_LANE, base + FOX_AUG_C + i] = 1.0
                pk[p, FOX_ONE_LANE, base + FOX_AUG_REF + i] = 1.0
                pq[p, base + FOX_AUG_C + i, FOX_PIECE_STRIDE * i + h] = 1.0
                pq[p, base + FOX_AUG_ONE + i, FOX_ONE_LANE] = 1.0
    return jnp.asarray(pk, BF16), jnp.asarray(pq, BF16)


def _fox_proj_body(x_ref, g_ref, w_ref, wf_ref, bf_ref, gq_ref, gk_ref, pk_ref, pq_ref,
                   qa_ref, ka_ref, va_ref, sg_ref, c_ref, hn_ref, carry_ref, *, hd, nh):
    tm, d = x_ref.shape[1], x_ref.shape[2]
    hn_ref[...] = _rmsnorm(x_ref[0], g_ref[...]).astype(BF16)

    @pl.when(pl.program_id(1) == 0)
    def _():
        carry_ref[...] = jnp.zeros_like(carry_ref)

    z = jnp.dot(hn_ref[...], wf_ref[...], preferred_element_type=F32) + bf_ref[...]
    lf = jnp.minimum(z, 0.0) - jnp.log1p(jnp.exp(-jnp.abs(z)))
    tri = (lax.broadcasted_iota(jnp.int32, (tm, tm), 1)
           <= lax.broadcasted_iota(jnp.int32, (tm, tm), 0)).astype(BF16)
    c = carry_ref[...]
    for piece in _split3(lf):
        c = c + jnp.dot(tri, piece.astype(BF16), preferred_element_type=F32)
    carry_ref[...] = c[tm - 1:tm, :]
    c_ref[0] = c[:, :nh]
    hi, mid, lo = _split3(c * LOG2E)
    lane = lax.broadcasted_iota(jnp.int32, (tm, LANES), 1)
    pieces = jnp.where(lane < FOX_PIECE_STRIDE, hi,
                       jnp.where(lane < 2 * FOX_PIECE_STRIDE, mid,
                                 jnp.where(lane < 3 * FOX_PIECE_STRIDE, lo,
                                           jnp.where(lane == FOX_ONE_LANE, 1.0, 0.0))))
    pieces_t = pieces.T.astype(BF16)
    pieces = pieces.astype(BF16)

    ri = lax.broadcasted_iota(jnp.int32, (LANES, LANES), 0)
    ci = lax.broadcasted_iota(jnp.int32, (LANES, LANES), 1)
    ones2 = ((ri // hd) == (ci // hd)).astype(BF16)
    ones_rows = jnp.ones((FOX_V_ROWS - hd, tm), F32)
    hpb = LANES // hd
    wide = 2 * LANES
    for part in range(4):
        for cw in range(d // wide):
            yw = jnp.dot(hn_ref[...], w_ref[:, part * d + cw * wide:part * d + (cw + 1) * wide],
                         preferred_element_type=F32)
            if part == 3:
                sg_ref[0, :, cw * wide:(cw + 1) * wide] = _sigmoid(yw)
                continue
            for half in range(wide // LANES):
                p = cw * (wide // LANES) + half
                y = yw[:, half * LANES:(half + 1) * LANES]
                if part == 1:
                    kn = _head_rmsnorm(y, gk_ref[...], ones2, hd)
                    aug = jnp.dot(pieces, pk_ref[p], preferred_element_type=F32)
                    for e in range(hpb):
                        ka_ref[0, p * hpb + e] = jnp.where((lane // hd) == e, kn,
                                                           aug[:, e * LANES:(e + 1) * LANES]).astype(BF16)
                    continue
                y_t = y.T
                if part == 2:
                    for e in range(hpb):
                        va_ref[0, p * hpb + e] = jnp.concatenate(
                            [y_t[e * hd:(e + 1) * hd], ones_rows], axis=0).astype(BF16)
                    continue
                aug_t = jnp.dot(pq_ref[p], pieces_t, preferred_element_type=F32)
                for e in range(hpb):
                    rows = y_t[e * hd:(e + 1) * hd]
                    ms = jnp.mean(rows * rows, axis=0, keepdims=True)
                    qn = rows * lax.rsqrt(ms + EPS) * gq_ref[...]
                    base = e * LANES + (hd if e == 0 else 0)
                    bias = aug_t[base:base + LANES - hd]
                    qa_ref[0, p * hpb + e] = jnp.concatenate([qn, bias] if e == 0 else [bias, qn],
                                                             axis=0).astype(BF16)


def _fox_proj(x, g, w_in, b_f, gq, gk):
    bsz, t, d = x.shape
    nh = FOX_HEADS
    hd = d // nh
    tm = min(512, t)
    pk, pq = _fox_placement(nh, hd)
    reps = FOX_ONE_LANE // FOX_PIECE_STRIDE
    w_f = jnp.pad(jnp.tile(jnp.pad(w_in[:, 4 * d:], ((0, 0), (0, FOX_PIECE_STRIDE - nh))), (1, reps)),
                  ((0, 0), (0, LANES - FOX_ONE_LANE)))
    b_pad = jnp.pad(jnp.tile(jnp.pad(b_f, (0, FOX_PIECE_STRIDE - nh)), reps), (0, LANES - FOX_ONE_LANE))
    row = pl.BlockSpec((1, tm, d), lambda b, i: (b, i, 0))
    return pl.pallas_call(
        functools.partial(_fox_proj_body, hd=hd, nh=nh),
        grid=(bsz, t // tm),
        in_specs=[row, _resident((1, d)), _resident((d, 4 * d)), _resident((d, LANES)),
                  _resident((1, LANES)), _resident((hd, 1)), _resident((1, LANES)),
                  _resident(pk.shape), _resident(pq.shape)],
        out_specs=[pl.BlockSpec((1, nh, LANES, tm), lambda b, i: (b, 0, 0, i)),
                   pl.BlockSpec((1, nh, tm, LANES), lambda b, i: (b, 0, i, 0)),
                   pl.BlockSpec((1, nh, FOX_V_ROWS, tm), lambda b, i: (b, 0, 0, i)),
                   row,
                   pl.BlockSpec((1, tm, nh), lambda b, i: (b, i, 0))],
        out_shape=[jax.ShapeDtypeStruct((bsz, nh, LANES, t), BF16),
                   jax.ShapeDtypeStruct((bsz, nh, t, LANES), BF16),
                   jax.ShapeDtypeStruct((bsz, nh, FOX_V_ROWS, t), BF16),
                   jax.ShapeDtypeStruct((bsz, t, d), F32),
                   jax.ShapeDtypeStruct((bsz, t, nh), F32)],
        scratch_shapes=[pltpu.VMEM((tm, d), BF16), pltpu.VMEM((1, LANES), F32)],
        compiler_params=_params("parallel", "arbitrary"),
        name="fox_proj",
    )(x, g.reshape(1, d), w_in[:, :4 * d].astype(BF16), w_f.astype(BF16), b_pad.reshape(1, LANES),
      gq.reshape(hd, 1), jnp.tile(gk, LANES // hd).reshape(1, LANES), pk, pq)


LOG2E = 1.4426950408889634
FOX_TQ = 512
FOX_TK = 256
FOX_SKIP_LOG = 105.0
FOX_FIXED_REF_MAX = 60.0


def _fox_blocks(lo_ref, qa_ref, ka_ref, e, *, tq, tk, hd, nh, ref_rows=None):
    b, pr, i = pl.program_id(0), pl.program_id(1), pl.program_id(2)
    spb = tq // tk
    head = pr * (LANES // hd) + e
    qa = qa_ref[0, e]
    if ref_rows is not None:
        r0 = (hd if e == 0 else 0) + FOX_AUG_REF
        qa = jnp.concatenate([qa[:r0], ref_rows, qa[r0 + ref_rows.shape[0]:]], axis=0)
    lo = lo_ref[(b * pl.num_programs(2) + i) * nh + head]
    rr = lax.broadcasted_iota(jnp.int32, (tk, tq), 0)
    cc = lax.broadcasted_iota(jnp.int32, (tk, tq), 1)

    def scores(j, masked):
        r0 = pl.multiple_of(j * tk, tk)
        s = jnp.dot(ka_ref[0, e, pl.ds(r0, tk), :], qa, preferred_element_type=F32)
        if masked:
            s = jnp.where(r0 + rr <= i * tq + cc, s, MASK_VALUE)
        return s

    return lo, i * spb, spb, scores


FOX_GROUP = 4


def _fox_attn_body(lo_ref, qa_ref, mr_ref, ka_ref, va_ref, sg_ref, o_ref,
                   qop_ref, s_ref, acc_ref, *, tq, tk, hd, nh):
    i = pl.program_id(2)
    hpb = LANES // hd
    spb = tq // tk
    n_key_blocks = pl.num_programs(2) * spb
    grp = s_ref.shape[0]
    los = []
    for e in range(hpb):
        lo, diag, _, _ = _fox_blocks(lo_ref, qa_ref, ka_ref, e, tq=tq, tk=tk, hd=hd, nh=nh)
        los.append(lo)
        r0 = (hd if e == 0 else 0) + FOX_AUG_REF
        qa = qa_ref[0, e]
        qop_ref[e] = jnp.concatenate([qa[:r0], mr_ref[0, e], qa[r0 + mr_ref.shape[2]:]], axis=0)
    counts = [diag + spb - lo for lo in los]
    n_items = functools.reduce(jnp.add, counts)
    acc_ref[...] = jnp.zeros_like(acc_ref)
    key_minus_query = (lax.broadcasted_iota(jnp.int32, (tk, tq), 0)
                       - lax.broadcasted_iota(jnp.int32, (tk, tq), 1))

    def item(w):
        e, j = jnp.int32(0), los[0] + w
        first = jnp.int32(0)
        for h in range(1, hpb):
            first = first + counts[h - 1]
            e = jnp.where(w >= first, h, e)
            j = jnp.where(w >= first, los[h] + (w - first), j)
        return e, j, jnp.minimum(j, n_key_blocks - 1), w < n_items

    def scores(w):
        e, _, jc, _ = item(w)
        kb = ka_ref[e, pl.ds(pl.multiple_of(jc * tk, tk), tk), :]
        return jnp.dot(kb, qop_ref[e], preferred_element_type=F32)

    def build_scores(w, slot):
        s_ref[slot] = scores(w)

    def consume(w, s):
        e, j, jc, real = item(w)
        limit = jnp.where(real, i * tq - j * tk, -tq)
        p = jnp.exp2(jnp.where(key_minus_query <= limit, s, MASK_VALUE)).astype(BF16)
        vb = va_ref[e, :, pl.ds(pl.multiple_of(jc * tk, tk), tk)]
        acc_ref[e] += jnp.dot(vb, p, preferred_element_type=F32)

    ka_ref, va_ref = ka_ref.at[0], va_ref.at[0]
    first = [scores(u) for u in range(grp)]
    for u in range(grp):
        build_scores(grp + u, u)
        consume(u, first[u])

    def stage(g, carry):
        for u in range(grp):
            s = s_ref[u]
            build_scores((g + 1) * grp + u, u)
            consume(g * grp + u, s)
        return carry

    lax.fori_loop(1, (n_items + grp - 1) // grp, stage, 0)
    outs = []
    for e in range(hpb):
        acc = acc_ref[e]
        outs.append(acc[:hd, :] / acc[hd:hd + 1, :])
    o = jnp.concatenate(outs, axis=0).T
    o_ref[0] = (o * sg_ref[0]).astype(o_ref.dtype)


def _fox_rowmax_body(lo_ref, qa_ref, ka_ref, m_ref, *, tq, tk, hd, nh):
    for e in range(LANES // hd):
        lo, diag, spb, scores = _fox_blocks(lo_ref, qa_ref, ka_ref, e, tq=tq, tk=tk, hd=hd, nh=nh)
        m = lax.fori_loop(lo, diag,
                          lambda j, m: jnp.maximum(m, jnp.max(scores(j, False), axis=0, keepdims=True)),
                          jnp.full((1, tq), MASK_VALUE, F32))
        for sb in range(spb):
            m = jnp.maximum(m, jnp.max(scores(diag + sb, True), axis=0, keepdims=True))
        m_ref[0, e] = jnp.broadcast_to(m, m_ref.shape[2:])


def _fox_attn(qa, ka, va, sg, c, bound):
    bsz, nh, _, t = qa.shape
    d = sg.shape[-1]
    hd = d // nh
    hpb = LANES // hd
    tq, tk = min(FOX_TQ, t), min(FOX_TK, t)
    c_start = c[:, 0::tq, :]
    c_end = c[:, tk - 1::tk, :]
    weightless = c_end[:, None, :, :] > (c_start + (2.0 * bound + FOX_SKIP_LOG))[:, :, None, :]
    first_block = jnp.sum(weightless, axis=2, dtype=jnp.int32).reshape(-1)
    smem = pl.BlockSpec(memory_space=pltpu.SMEM)
    grid = (bsz, nh // hpb, t // tq)
    qblk = pl.BlockSpec((1, hpb, LANES, tq), lambda b, p, i: (b, p, 0, i))
    kblk = pl.BlockSpec((1, hpb, t, LANES), lambda b, p, i: (b, p, 0, 0))
    static = dict(tq=tq, tk=tk, hd=hd, nh=nh)
    bound2 = bound * LOG2E

    def true_rowmax():
        m = pl.pallas_call(
            functools.partial(_fox_rowmax_body, **static),
            grid=grid,
            in_specs=[smem, qblk, kblk],
            out_specs=pl.BlockSpec((1, hpb, 8, tq), lambda b, p, i: (b, p, 0, i)),
            out_shape=jax.ShapeDtypeStruct((bsz, nh, 8, t), F32),
            compiler_params=_params("parallel", "parallel", "arbitrary"),
            name="fox_rowmax",
        )(first_block, qa, ka)
        return m[:, :, 0, :]

    ref = lax.cond(2.0 * bound2 <= FOX_FIXED_REF_MAX,
                   lambda: jnp.full((bsz, nh, t), bound2, F32), true_rowmax)
    pieces = jnp.stack([-p for p in _split3(ref)], axis=2)
    mr = jnp.pad(pieces, ((0, 0), (0, 0), (0, 16 - 3), (0, 0))).astype(BF16)
    oblk = pl.BlockSpec((1, tq, LANES), lambda b, p, i: (b, i, p))
    return pl.pallas_call(
        functools.partial(_fox_attn_body, **static),
        grid=grid,
        in_specs=[smem, qblk,
                  pl.BlockSpec((1, hpb, 16, tq), lambda b, p, i: (b, p, 0, i)),
                  kblk,
                  pl.BlockSpec((1, hpb, FOX_V_ROWS, t), lambda b, p, i: (b, p, 0, 0)),
                  oblk],
        out_specs=oblk,
        out_shape=jax.ShapeDtypeStruct((bsz, t, d), BF16),
        scratch_shapes=[pltpu.VMEM((hpb, LANES, tq), BF16), pltpu.VMEM((FOX_GROUP, tk, tq), F32),
                        pltpu.VMEM((hpb, FOX_V_ROWS, tq), F32)],
        compiler_params=_params("parallel", "parallel", "arbitrary"),
        name="fox_attn",
    )(first_block, qa, mr, ka, va, sg)


def _fox_mixer(x, bsz, t, g, w_in, b_f, gq, gk):
    n, d = x.shape
    hd = d // FOX_HEADS
    gq_s = gq * (LOG2E / math.sqrt(hd))
    qa, ka, va, sg, c = _fox_proj(x.reshape(bsz, t, d), g, w_in, b_f, gq_s, gk)
    bound = 1.02 * hd * jnp.max(jnp.abs(gq)) / math.sqrt(hd) * jnp.max(jnp.abs(gk))
    return _fox_attn(qa, ka, va, sg, c, bound).reshape(n, d)


def kernel(x, norm_g, ffn_w_in, ffn_w_out, hg_w_in, hg_lb_logits, hg_out_norm_g, hg_w_out,
           fox_w_in, fox_b_f, fox_q_norm_g, fox_k_norm_g, fox_w_out):
    bsz, t, d = x.shape
    depth = norm_g.shape[0]
    h = x.reshape(bsz * t, d)
    for layer in range(depth):
        j = layer // N_MIXERS
        h = _ffn(h, norm_g[layer, 0], ffn_w_in[layer, 0], ffn_w_out[layer, 0])
        if layer % N_MIXERS == 0:
            o = _hgrn_mixer(h, bsz, t, norm_g[layer, 1], hg_w_in[j], hg_lb_logits, j, hg_out_norm_g[j])
            w_o = hg_w_out[j]
        else:
            o = _fox_mixer(h, bsz, t, norm_g[layer, 1], fox_w_in[j], fox_b_f[j],
                           fox_q_norm_g[j], fox_k_norm_g[j])
            w_o = fox_w_out[j]
        h = _ffn(h, norm_g[layer, 2], ffn_w_in[layer, 1], ffn_w_out[layer, 1], mixer=(o, w_o))
    return h.reshape(bsz, t, d)
```
